```python
import math
import jax, jax.numpy as jnp
from jax import lax
import numpy as np

D_MODEL = 2048
BATCH = 8
SEQ = 2048
DEPTH = 1
DEC_BATCH = 2
DEC_SEQ = 16384
PAST_LEN = 128

A_HEADS = 8
A_QK_DIM = 64
A_V_DIM = 2 * A_QK_DIM
A_QK_W = A_HEADS * 2 * A_QK_DIM
A_V_W = A_HEADS * A_V_DIM
B_GROUPS = ((128, 1), (512, 4), (2048, 16))
B_HEADS_PER_GROUP = 4
B_HEADS = B_HEADS_PER_GROUP * len(B_GROUPS)
B_HEAD_DIM = 128
B_W = B_HEADS * B_HEAD_DIM
B_OUT_W = B_HEADS_PER_GROUP * B_HEAD_DIM
BAND_BLOCK = max(w // (2 * r) for (w, r) in B_GROUPS)
D_FF = 4 * D_MODEL
ROPE_THETA = 500000.0
ROPE_FRACTION_DEN = 4
Q_BLOCK = 128
NORM_EPS = 1e-6
SUBLN_EPS = 1e-5
NEG_BIG = -1e30
IN_SPLITS = (A_QK_W, A_QK_W, A_V_W, B_W, B_W, B_W, D_MODEL, D_MODEL)
IN_WIDTH = sum(IN_SPLITS)

kernel_name = "hybrid_diff_dilated_gated_encoder"


def rmsnorm(x, g, eps):
    xf = x.astype(jnp.float32)
    y = xf * lax.rsqrt(jnp.mean(xf * xf, axis=-1, keepdims=True) + eps)
    return (y * g.astype(jnp.float32)).astype(x.dtype)


def partial_rope(x, rot_dim):
    S = x.shape[1]
    half = rot_dim // 2
    pos = jnp.arange(S, dtype=jnp.float32)
    inv = ROPE_THETA ** (-jnp.arange(0, rot_dim, 2, dtype=jnp.float32) / rot_dim)
    ang = pos[:, None] * inv[None, :]
    cos = jnp.cos(ang)[None, :, None, :].astype(x.dtype)
    sin = jnp.sin(ang)[None, :, None, :].astype(x.dtype)
    x1 = x[..., :half]
    x2 = x[..., half:rot_dim]
    return jnp.concatenate([x1 * cos - x2 * sin, x2 * cos + x1 * sin, x[..., rot_dim:]], axis=-1)


def diff_attention(q, k, v, lam, subln_g, lambda_init):
    Bsz, S, H, _, d = q.shape
    nq = S // Q_BLOCK
    q = q * (d ** -0.5)
    qb = q.reshape(Bsz, nq, Q_BLOCK, H, 2, d).transpose(1, 0, 2, 3, 4, 5)

    def block(qblk):
        s = jnp.einsum('bqhcd,bkhcd->bhcqk', qblk, k).astype(jnp.float32)
        p = jax.nn.softmax(s, axis=-1)
        a = p[:, :, 0] - lam * p[:, :, 1]
        return jnp.einsum('bhqk,bkhe->bqhe', a.astype(v.dtype), v)

    o = lax.map(block, qb)
    o = o.transpose(1, 0, 2, 3, 4).reshape(Bsz, S, H, 2 * d)
    o = rmsnorm(o, subln_g, SUBLN_EPS) * (1.0 - lambda_init)
    return o.reshape(Bsz, S, H * 2 * d)


def dilated_group(q, k, v, dilation, half):
    Bsz, S, Hg, dh = q.shape
    L = S // dilation
    N = Bsz * dilation

    def to_sub(t):
        return t.reshape(Bsz, L, dilation, Hg, dh).transpose(0, 2, 1, 3, 4).reshape(N, L, Hg, dh)

    qs, ks, vs = to_sub(q) * (dh ** -0.5), to_sub(k), to_sub(v)
    nb = -(-L // BAND_BLOCK)
    Lp = nb * BAND_BLOCK
    qs = jnp.pad(qs, ((0, 0), (0, Lp - L), (0, 0), (0, 0)))
    kv_pad = ((0, 0), (BAND_BLOCK, Lp - L + BAND_BLOCK), (0, 0), (0, 0))
    kp = jnp.pad(ks, kv_pad).reshape(N, nb + 2, BAND_BLOCK, Hg, dh)
    vp = jnp.pad(vs, kv_pad).reshape(N, nb + 2, BAND_BLOCK, Hg, dh)
    kw = jnp.concatenate([kp[:, :-2], kp[:, 1:-1], kp[:, 2:]], axis=2)
    vw = jnp.concatenate([vp[:, :-2], vp[:, 1:-1], vp[:, 2:]], axis=2)
    qb = qs.reshape(N, nb, BAND_BLOCK, Hg, dh)

    s = jnp.einsum('nbqhd,nbkhd->nbhqk', qb, kw).astype(jnp.float32)
    qpos = jnp.arange(nb)[:, None] * BAND_BLOCK + jnp.arange(BAND_BLOCK)[None, :]
    kpos = (jnp.arange(nb)[:, None] - 1) * BAND_BLOCK + jnp.arange(3 * BAND_BLOCK)[None, :]
    rel = kpos[:, None, :] - qpos[:, :, None]
    mask = (jnp.abs(rel) <= half) & (kpos[:, None, :] >= 0) & (kpos[:, None, :] < L)
    s = jnp.where(mask[None, :, None], s, NEG_BIG)
    m = jnp.max(s, axis=-1, keepdims=True)
    p = jnp.exp(s - m)
    l = jnp.sum(p, axis=-1, keepdims=True)
    o = jnp.einsum('nbhqk,nbkhd->nbqhd', (p / l).astype(v.dtype), vw)
    lse = (m + jnp.log(l))[..., 0].transpose(0, 1, 3, 2)

    o = o.reshape(N, Lp, Hg, dh)[:, :L]
    lse = lse.reshape(N, Lp, Hg)[:, :L]
    o = o.reshape(Bsz, dilation, L, Hg, dh).transpose(0, 2, 1, 3, 4).reshape(Bsz, S, Hg, dh)
    lse = lse.reshape(Bsz, dilation, L, Hg).transpose(0, 2, 1, 3).reshape(Bsz, S, Hg)
    return o, lse


def dilated_attention(q, k, v):
    Bsz, S = q.shape[:2]
    outs, lses = [], []
    for g, (window, dilation) in enumerate(B_GROUPS):
        sl = slice(g * B_HEADS_PER_GROUP, (g + 1) * B_HEADS_PER_GROUP)
        o, lse = dilated_group(q[:, :, sl], k[:, :, sl], v[:, :, sl], dilation, window // (2 * dilation))
        outs.append(o)
        lses.append(lse)
    w = jax.nn.softmax(jnp.stack(lses, axis=0), axis=0)
    o = jnp.sum(w[..., None] * jnp.stack(outs, axis=0).astype(jnp.float32), axis=0)
    return o.astype(q.dtype).reshape(Bsz, S, B_OUT_W)


def trunk(x, norm_mix, w_in, lambda_q1, lambda_k1, lambda_q2, lambda_k2, subln_g,
          w_proj_a, w_proj_b, w_out, norm_ffn, w1, w2, norm_final):
    Bsz, S, _ = x.shape
    split_idx = [int(i) for i in np.cumsum(IN_SPLITS)[:-1]]
    for l in range(DEPTH):
        lambda_init = 0.8 - 0.6 * math.exp(-0.3 * l)
        h = rmsnorm(x, norm_mix[l], NORM_EPS)
        z = h @ w_in[l]
        qa, ka, va, qb, kb, vb, ga, gb = jnp.split(z, split_idx, axis=-1)
        qa = partial_rope(qa.reshape(Bsz, S, 2 * A_HEADS, A_QK_DIM), A_QK_DIM // ROPE_FRACTION_DEN)
        ka = partial_rope(ka.reshape(Bsz, S, 2 * A_HEADS, A_QK_DIM), A_QK_DIM // ROPE_FRACTION_DEN)
        qa = qa.reshape(Bsz, S, A_HEADS, 2, A_QK_DIM)
        ka = ka.reshape(Bsz, S, A_HEADS, 2, A_QK_DIM)
        va = va.reshape(Bsz, S, A_HEADS, A_V_DIM)
        lam = (jnp.exp(jnp.sum(lambda_q1[l].astype(jnp.float32) * lambda_k1[l].astype(jnp.float32)))
               - jnp.exp(jnp.sum(lambda_q2[l].astype(jnp.float32) * lambda_k2[l].astype(jnp.float32)))
               + lambda_init)
        ya = diff_attention(qa, ka, va, lam, subln_g[l], lambda_init) @ w_proj_a[l]
        qb = partial_rope(qb.reshape(Bsz, S, B_HEADS, B_HEAD_DIM), B_HEAD_DIM // ROPE_FRACTION_DEN)
        kb = partial_rope(kb.reshape(Bsz, S, B_HEADS, B_HEAD_DIM), B_HEAD_DIM // ROPE_FRACTION_DEN)
        vb = vb.reshape(Bsz, S, B_HEADS, B_HEAD_DIM)
        yb = dilated_attention(qb, kb, vb) @ w_proj_b[l]
        merged = jax.nn.sigmoid(ga) * ya + jax.nn.sigmoid(gb) * yb
        x = x + merged @ w_out[l]
        h = rmsnorm(x, norm_ffn[l], NORM_EPS)
        x = x + jnp.square(jax.nn.relu(h @ w1[l])) @ w2[l]
    return rmsnorm(x, norm_final, NORM_EPS)


def setup_inputs(seed: int = 0) -> dict:
    key = jax.random.key(seed)
    ks = jax.random.split(key, 17)
    f32 = jnp.float32

    def nrm(k, shape, scale):
        return jax.random.normal(k, shape, f32) * scale

    return {
        "x_prompt": nrm(ks[0], (BATCH, SEQ, D_MODEL), 1.0),
        "x_sample": nrm(ks[1], (DEC_BATCH, DEC_SEQ, D_MODEL), 1.0),
        "norm_mix": 1.0 + nrm(ks[2], (DEPTH, D_MODEL), 0.02),
        "w_in": nrm(ks[3], (DEPTH, D_MODEL, IN_WIDTH), D_MODEL ** -0.5),
        "lambda_q1": nrm(ks[4], (DEPTH, A_QK_DIM), 0.1),
        "lambda_k1": nrm(ks[5], (DEPTH, A_QK_DIM), 0.1),
        "lambda_q2": nrm(ks[6], (DEPTH, A_QK_DIM), 0.1),
        "lambda_k2": nrm(ks[7], (DEPTH, A_QK_DIM), 0.1),
        "subln_g": 1.0 + nrm(ks[8], (DEPTH, A_V_DIM), 0.02),
        "w_proj_a": nrm(ks[9], (DEPTH, A_V_W, D_MODEL), A_V_W ** -0.5),
        "w_proj_b": nrm(ks[10], (DEPTH, B_OUT_W, D_MODEL), B_OUT_W ** -0.5),
        "w_out": nrm(ks[11], (DEPTH, D_MODEL, D_MODEL), D_MODEL ** -0.5),
        "norm_ffn": 1.0 + nrm(ks[12], (DEPTH, D_MODEL), 0.02),
        "w1": nrm(ks[13], (DEPTH, D_MODEL, D_FF), D_MODEL ** -0.5),
        "w2": nrm(ks[14], (DEPTH, D_FF, D_MODEL), D_FF ** -0.5),
        "norm_final": 1.0 + nrm(ks[15], (D_MODEL,), 0.02),
    }


def reference(x_prompt, x_sample, norm_mix, w_in, lambda_q1, lambda_k1, lambda_q2, lambda_k2,
              subln_g, w_proj_a, w_proj_b, w_out, norm_ffn, w1, w2, norm_final):
    y_prompt = trunk(x_prompt, norm_mix, w_in, lambda_q1, lambda_k1, lambda_q2, lambda_k2, subln_g,
                     w_proj_a, w_proj_b, w_out, norm_ffn, w1, w2, norm_final)
    y_sample = trunk(x_sample, norm_mix, w_in, lambda_q1, lambda_k1, lambda_q2, lambda_k2, subln_g,
                     w_proj_a, w_proj_b, w_out, norm_ffn, w1, w2, norm_final)
    return (y_prompt, y_sample)
```

```python
import functools
import math

import jax
import jax.numpy as jnp
import numpy as np
from jax import lax
from jax.experimental import pallas as pl
from jax.experimental.pallas import tpu as pltpu

D_MODEL = 2048
A_HEADS = 8
A_QK_DIM = 64
A_V_DIM = 2 * A_QK_DIM
B_GROUPS = ((128, 1), (512, 4), (2048, 16))
B_HEADS_PER_GROUP = 4
B_HEAD_DIM = 128
B_OUT_W = B_HEADS_PER_GROUP * B_HEAD_DIM
D_FF = 4 * D_MODEL
ROPE_THETA = 500000.0
ROPE_FRACTION_DEN = 4
NORM_EPS = 1e-6
SUBLN_EPS = 1e-5
NEG_BIG = -1e30

COL_BLK = 512
IN_WIDTH = 11776
N_COL_BLKS = IN_WIDTH // COL_BLK
QA_BLK, KA_BLK, VA_BLK = 0, 2, 4
QB_BLK, KB_BLK, VB_BLK = 6, 9, 12
GA_BLK, GB_BLK = 15, 19
LANES = 128
BAND_HALF = 64
BAND_SUB = 128

VMEM_LIMIT = 56 * 1024 * 1024

BF16 = jnp.bfloat16
F32 = jnp.float32


def _params(n_axes):
    return pltpu.CompilerParams(dimension_semantics=("arbitrary",) * n_axes,
                                vmem_limit_bytes=VMEM_LIMIT)


def _rope_tables(seq, head_dim):
    rot = head_dim // ROPE_FRACTION_DEN
    half = rot // 2
    pos = jnp.arange(seq, dtype=F32)
    inv = ROPE_THETA ** (-jnp.arange(0, rot, 2, dtype=F32) / rot)
    ang = pos[:, None] * inv[None, :]
    cos, sin = jnp.cos(ang), jnp.sin(ang)
    pad = head_dim - rot
    cos_h = jnp.concatenate([cos, cos, jnp.ones((seq, pad), F32)], axis=1)
    sin_h = jnp.concatenate([-sin, sin, jnp.zeros((seq, pad), F32)], axis=1)
    reps = LANES // head_dim
    return jnp.tile(cos_h, (1, reps)), jnp.tile(sin_h, (1, reps))


def _rope_store(acc, cos, sin, head_dim, scale, z_ref):
    half = head_dim // ROPE_FRACTION_DEN // 2
    lane = lax.broadcasted_iota(jnp.int32, (1, LANES), 1)
    first_half = (lane % head_dim) < half
    for c in range(COL_BLK // LANES):
        xc = acc[:, c * LANES:(c + 1) * LANES]
        up = pltpu.roll(xc, LANES - half, 1)
        dn = pltpu.roll(xc, half, 1)
        out = xc * cos + jnp.where(first_half, up, dn) * sin
        if scale != 1.0:
            out = out * scale
        z_ref[:, c * LANES:(c + 1) * LANES] = out.astype(z_ref.dtype)


def _in_proj_kernel(x_ref, g_ref, w_ref, ca_ref, sa_ref, cb_ref, sb_ref, z_ref, h_ref):
    j = pl.program_id(1)

    @pl.when(j == 0)
    def _():
        x = x_ref[...]
        ms = jnp.mean(x * x, axis=-1, keepdims=True)
        h_ref[...] = (x * lax.rsqrt(ms + NORM_EPS) * g_ref[...]).astype(h_ref.dtype)

    acc = jnp.dot(h_ref[...], w_ref[...], preferred_element_type=F32)

    @pl.when(j < KA_BLK)
    def _():
        _rope_store(acc, ca_ref[...], sa_ref[...], A_QK_DIM, A_QK_DIM ** -0.5, z_ref)

    @pl.when((j >= KA_BLK) & (j < VA_BLK))
    def _():
        _rope_store(acc, ca_ref[...], sa_ref[...], A_QK_DIM, 1.0, z_ref)

    @pl.when((j >= QB_BLK) & (j < KB_BLK))
    def _():
        _rope_store(acc, cb_ref[...], sb_ref[...], B_HEAD_DIM, B_HEAD_DIM ** -0.5, z_ref)

    @pl.when((j >= KB_BLK) & (j < VB_BLK))
    def _():
        _rope_store(acc, cb_ref[...], sb_ref[...], B_HEAD_DIM, 1.0, z_ref)

    @pl.when(((j >= VA_BLK) & (j < QB_BLK)) | (j >= VB_BLK))
    def _():
        z_ref[...] = acc.astype(z_ref.dtype)


def _in_proj(x2d, seq, g, w_bf16, tabs, tm):
    tokens = x2d.shape[0]
    tab_spec = pl.BlockSpec((tm, LANES), lambda i, j: (i % (seq // tm), 0))
    return pl.pallas_call(
        _in_proj_kernel,
        grid=(tokens // tm, N_COL_BLKS),
        in_specs=[
            pl.BlockSpec((tm, D_MODEL), lambda i, j: (i, 0)),
            pl.BlockSpec((1, D_MODEL), lambda i, j: (0, 0)),
            pl.BlockSpec((D_MODEL, COL_BLK), lambda i, j: (0, j)),
            tab_spec, tab_spec, tab_spec, tab_spec,
        ],
        out_specs=pl.BlockSpec((tm, COL_BLK), lambda i, j: (i, j)),
        out_shape=jax.ShapeDtypeStruct((tokens, IN_WIDTH), BF16),
        scratch_shapes=[pltpu.VMEM((tm, D_MODEL), BF16)],
        compiler_params=_params(2),
        name="in_proj",
    )(x2d, g, w_bf16, *tabs)


def _diff_attn_kernel(q_ref, k_ref, v_ref, lq1_ref, lk1_ref, lq2_ref, lk2_ref, g_ref, o_ref,
                      q2_ref, vext_ref, m_ref, acc_ref, *, tq, tk, seq, lambda_init):
    qi = pl.program_id(2)

    @pl.when(qi == 0)
    def _():
        vext_ref[:, :A_V_DIM] = v_ref[...]
        vext_ref[:, A_V_DIM:] = jnp.ones((seq, A_V_DIM), vext_ref.dtype)

    q = q_ref[...]
    lane = lax.broadcasted_iota(jnp.int32, (1, LANES), 1)
    zero = jnp.zeros_like(q)
    q2_ref[:tq, :] = jnp.where(lane < A_QK_DIM, q, zero)
    q2_ref[tq:, :] = jnp.where(lane >= A_QK_DIM, q, zero)
    m_ref[...] = jnp.full(m_ref.shape, NEG_BIG, F32)
    acc_ref[...] = jnp.zeros(acc_ref.shape, F32)

    def chunk(c, carry):
        start = pl.multiple_of(c * tk, tk)
        kc = k_ref[pl.ds(start, tk), :]
        s = lax.dot_general(q2_ref[...], kc, (((1,), (1,)), ((), ())), preferred_element_type=F32)
        m_old = m_ref[...]
        m_new = jnp.maximum(m_old, jnp.max(s, axis=-1, keepdims=True))
        p = jnp.exp(s - m_new[:, :1]).astype(BF16)
        pv = jnp.dot(p, vext_ref[pl.ds(start, tk), :], preferred_element_type=F32)
        alpha = jnp.exp(m_old - m_new)
        acc_ref[...] = jnp.concatenate([alpha, alpha], axis=1) * acc_ref[...] + pv
        m_ref[...] = m_new
        return carry

    lax.fori_loop(0, seq // tk, chunk, 0)

    acc = acc_ref[...]
    o_maps = acc[:, :A_V_DIM] / acc[:, A_V_DIM:]
    lam = (jnp.exp(jnp.sum(lq1_ref[...] * lk1_ref[...], axis=-1, keepdims=True))
           - jnp.exp(jnp.sum(lq2_ref[...] * lk2_ref[...], axis=-1, keepdims=True)) + lambda_init)
    o = o_maps[:tq] - lam * o_maps[tq:]
    ms = jnp.mean(o * o, axis=-1, keepdims=True)
    o = o * lax.rsqrt(ms + SUBLN_EPS) * g_ref[...] * (1.0 - lambda_init)
    o_ref[...] = o.astype(o_ref.dtype)


def _diff_attn(z, batch, seq, lq1, lk1, lq2, lk2, subln_g, lambda_init, tq, tk):
    tokens = batch * seq
    nq = seq // tq
    vec = lambda n: pl.BlockSpec((1, n), lambda b, h, qi: (0, 0))
    kern = functools.partial(_diff_attn_kernel, tq=tq, tk=tk, seq=seq, lambda_init=lambda_init)
    return pl.pallas_call(
        kern,
        grid=(batch, A_HEADS, nq),
        in_specs=[
            pl.BlockSpec((tq, A_V_DIM), lambda b, h, qi: (b * nq + qi, h)),
            pl.BlockSpec((seq, A_V_DIM), lambda b, h, qi: (b, A_HEADS + h)),
            pl.BlockSpec((seq, A_V_DIM), lambda b, h, qi: (b, 2 * A_HEADS + h)),
            vec(A_QK_DIM), vec(A_QK_DIM), vec(A_QK_DIM), vec(A_QK_DIM), vec(A_V_DIM),
        ],
        out_specs=pl.BlockSpec((tq, A_V_DIM), lambda b, h, qi: (b * nq + qi, h)),
        out_shape=jax.ShapeDtypeStruct((tokens, A_HEADS * A_V_DIM), BF16),
        scratch_shapes=[
            pltpu.VMEM((2 * tq, A_V_DIM), BF16),
            pltpu.VMEM((seq, 2 * A_V_DIM), BF16),
            pltpu.VMEM((2 * tq, LANES), F32),
            pltpu.VMEM((2 * tq, 2 * A_V_DIM), F32),
        ],
        compiler_params=_params(3),
        name="diff_attn",
    )(z, z, z, lq1, lk1, lq2, lk2, subln_g)


def _band_attn_kernel(q_ref, kp_ref, kc_ref, kn_ref, vp_ref, vc_ref, vn_ref, o_ref, lse_ref,
                      kbuf_ref, vbuf_ref, *, tl, length):
    jb = pl.program_id(2)
    kbuf_ref[:BAND_HALF] = kp_ref[...]
    kbuf_ref[BAND_HALF:BAND_HALF + tl] = kc_ref[...]
    kbuf_ref[BAND_HALF + tl:] = kn_ref[...]
    vbuf_ref[:BAND_HALF] = vp_ref[...]
    vbuf_ref[BAND_HALF:BAND_HALF + tl] = vc_ref[...]
    vbuf_ref[BAND_HALF + tl:] = vn_ref[...]

    nkeys = BAND_SUB + 2 * BAND_HALF
    rows = lax.broadcasted_iota(jnp.int32, (BAND_SUB, nkeys), 0)
    cols = lax.broadcasted_iota(jnp.int32, (BAND_SUB, nkeys), 1)
    in_band = jnp.abs(cols - BAND_HALF - rows) <= BAND_HALF

    def sub_block(i, carry):
        r0 = pl.multiple_of(i * BAND_SUB, BAND_SUB)
        kpos = jb * tl + r0 - BAND_HALF + cols
        mask = in_band & (kpos >= 0) & (kpos < length)
        for h in range(B_HEADS_PER_GROUP):
            hs = slice(h * B_HEAD_DIM, (h + 1) * B_HEAD_DIM)
            q = q_ref[pl.ds(r0, BAND_SUB), hs]
            k = kbuf_ref[pl.ds(r0, nkeys), hs]
            v = vbuf_ref[pl.ds(r0, nkeys), hs]
            s = lax.dot_general(q, k, (((1,), (1,)), ((), ())), preferred_element_type=F32)
            s = jnp.where(mask, s, NEG_BIG)
            m = jnp.max(s, axis=-1, keepdims=True)
            p = jnp.exp(s - m)
            l = jnp.sum(p, axis=-1, keepdims=True)
            o = jnp.dot(p.astype(BF16), v, preferred_element_type=F32) / l
            o_ref[pl.ds(r0, BAND_SUB), hs] = o.astype(o_ref.dtype)
            lse_ref[pl.ds(r0, BAND_SUB), hs] = jnp.broadcast_to(m + jnp.log(l), (BAND_SUB, B_HEAD_DIM))
        return carry

    lax.fori_loop(0, tl // BAND_SUB, sub_block, 0)


def _band_attn(z, batch, seq, group):
    dilation = B_GROUPS[group][1]
    assert B_GROUPS[group][0] // (2 * dilation) == BAND_HALF
    length = seq // dilation
    tl = min(length, 1024)
    nhalo = tl // BAND_HALF
    row_w = N_COL_BLKS * dilation
    zr = z.reshape(batch, length, dilation * IN_WIDTH)

    def cur(blk):
        return pl.BlockSpec((None, tl, COL_BLK), lambda b, r, jb: (b, jb, r * N_COL_BLKS + blk + group))

    def prev(blk):
        return pl.BlockSpec((None, BAND_HALF, COL_BLK),
                            lambda b, r, jb: (b, jnp.maximum(jb * nhalo - 1, 0), r * N_COL_BLKS + blk + group))

    def nxt(blk):
        return pl.BlockSpec((None, BAND_HALF, COL_BLK),
                            lambda b, r, jb: (b, jnp.minimum((jb + 1) * nhalo, length // BAND_HALF - 1),
                                              r * N_COL_BLKS + blk + group))

    del row_w
    out_spec = pl.BlockSpec((None, tl, B_OUT_W), lambda b, r, jb: (b, jb, r))
    kern = functools.partial(_band_attn_kernel, tl=tl, length=length)
    o, lse = pl.pallas_call(
        kern,
        grid=(batch, dilation, length // tl),
        in_specs=[cur(QB_BLK), prev(KB_BLK), cur(KB_BLK), nxt(KB_BLK), prev(VB_BLK), cur(VB_BLK), nxt(VB_BLK)],
        out_specs=[out_spec, out_spec],
        out_shape=[jax.ShapeDtypeStruct((batch, length, dilation * B_OUT_W), BF16),
                   jax.ShapeDtypeStruct((batch, length, dilation * B_OUT_W), F32)],
        scratch_shapes=[pltpu.VMEM((tl + 2 * BAND_HALF, COL_BLK), BF16),
                        pltpu.VMEM((tl + 2 * BAND_HALF, COL_BLK), BF16)],
        compiler_params=_params(3),
        name=f"band_attn_g{group}",
    )(zr, zr, zr, zr, zr, zr, zr)
    return o.reshape(batch * seq, B_OUT_W), lse.reshape(batch * seq, B_OUT_W)


def _merge_out_kernel(x_ref, oa_ref, o0_ref, o1_ref, o2_ref, l0_ref, l1_ref, l2_ref, ga_ref, gb_ref,
                      wpa_ref, wpb_ref, wout_ref, y_ref, ob_ref):
    c = pl.program_id(1)

    @pl.when(c == 0)
    def _():
        l0, l1, l2 = l0_ref[...], l1_ref[...], l2_ref[...]
        mx = jnp.maximum(jnp.maximum(l0, l1), l2)
        e0, e1, e2 = jnp.exp(l0 - mx), jnp.exp(l1 - mx), jnp.exp(l2 - mx)
        num = (e0 * o0_ref[...].astype(F32) + e1 * o1_ref[...].astype(F32) + e2 * o2_ref[...].astype(F32))
        ob_ref[...] = (num / (e0 + e1 + e2)).astype(ob_ref.dtype)
        y_ref[...] = x_ref[...]

    ya = jnp.dot(oa_ref[...], wpa_ref[...], preferred_element_type=F32)
    yb = jnp.dot(ob_ref[...], wpb_ref[...], preferred_element_type=F32)
    merged = (jax.nn.sigmoid(ga_ref[...].astype(F32)) * ya + jax.nn.sigmoid(gb_ref[...].astype(F32)) * yb)
    y_ref[...] += jnp.dot(merged.astype(BF16), wout_ref[...], preferred_element_type=F32)


def _merge_out(x2d, oa, ob_groups, z, wpa, wpb, wout, tm):
    tokens = x2d.shape[0]
    row = lambda w: pl.BlockSpec((tm, w), lambda i, c: (i, 0))
    n_c = D_MODEL // COL_BLK
    (o0, l0), (o1, l1), (o2, l2) = ob_groups
    return pl.pallas_call(
        _merge_out_kernel,
        grid=(tokens // tm, n_c),
        in_specs=[
            row(D_MODEL), row(A_HEADS * A_V_DIM),
            row(B_OUT_W), row(B_OUT_W), row(B_OUT_W), row(B_OUT_W), row(B_OUT_W), row(B_OUT_W),
            pl.BlockSpec((tm, COL_BLK), lambda i, c: (i, GA_BLK + c)),
            pl.BlockSpec((tm, COL_BLK), lambda i, c: (i, GB_BLK + c)),
            pl.BlockSpec((A_HEADS * A_V_DIM, COL_BLK), lambda i, c: (0, c)),
            pl.BlockSpec((B_OUT_W, COL_BLK), lambda i, c: (0, c)),
            pl.BlockSpec((COL_BLK, D_MODEL), lambda i, c: (c, 0)),
        ],
        out_specs=row(D_MODEL),
        out_shape=jax.ShapeDtypeStruct((tokens, D_MODEL), F32),
        scratch_shapes=[pltpu.VMEM((tm, B_OUT_W), BF16)],
        compiler_params=_params(2),
        name="merge_out",
    )(x2d, oa, o0, o1, o2, l0, l1, l2, z, z, wpa, wpb, wout)


def _ffn_kernel(x_ref, g_ref, w1_ref, w2_ref, gf_ref, y_ref, h_ref, acc_ref, *, final_norm):
    f = pl.program_id(1)

    @pl.when(f == 0)
    def _():
        x = x_ref[...]
        ms = jnp.mean(x * x, axis=-1, keepdims=True)
        h_ref[...] = (x * lax.rsqrt(ms + NORM_EPS) * g_ref[...]).astype(h_ref.dtype)
        acc_ref[...] = x

    u = jnp.dot(h_ref[...], w1_ref[...], preferred_element_type=F32)
    u = jnp.square(jnp.maximum(u, 0.0)).astype(BF16)
    acc_ref[...] += jnp.dot(u, w2_ref[...], preferred_element_type=F32)

    @pl.when(f == pl.num_programs(1) - 1)
    def _():
        y = acc_ref[...]
        if final_norm:
            ms = jnp.mean(y * y, axis=-1, keepdims=True)
            y = y * lax.rsqrt(ms + NORM_EPS) * gf_ref[...]
        y_ref[...] = y


def _ffn(x2d, g, w1, w2, g_final, final_norm, tm, tf):
    tokens = x2d.shape[0]
    kern = functools.partial(_ffn_kernel, final_norm=final_norm)
    return pl.pallas_call(
        kern,
        grid=(tokens // tm, D_FF // tf),
        in_specs=[
            pl.BlockSpec((tm, D_MODEL), lambda i, f: (i, 0)),
            pl.BlockSpec((1, D_MODEL), lambda i, f: (0, 0)),
            pl.BlockSpec((D_MODEL, tf), lambda i, f: (0, f)),
            pl.BlockSpec((tf, D_MODEL), lambda i, f: (f, 0)),
            pl.BlockSpec((1, D_MODEL), lambda i, f: (0, 0)),
        ],
        out_specs=pl.BlockSpec((tm, D_MODEL), lambda i, f: (i, 0)),
        out_shape=jax.ShapeDtypeStruct((tokens, D_MODEL), F32),
        scratch_shapes=[pltpu.VMEM((tm, D_MODEL), BF16), pltpu.VMEM((tm, D_MODEL), F32)],
        compiler_params=_params(2),
        name="ffn",
    )(x2d, g, w1, w2, g_final)


def _trunk(x, norm_mix, w_in, lambda_q1, lambda_k1, lambda_q2, lambda_k2, subln_g,
           w_proj_a, w_proj_b, w_out, norm_ffn, w1, w2, norm_final):
    batch, seq, _ = x.shape
    depth = w_in.shape[0]
    x2d = x.reshape(batch * seq, D_MODEL)
    tabs = _rope_tables(seq, A_QK_DIM) + _rope_tables(seq, B_HEAD_DIM)
    row = lambda v: v.reshape(1, -1)
    for l in range(depth):
        lambda_init = 0.8 - 0.6 * math.exp(-0.3 * l)
        z = _in_proj(x2d, seq, row(norm_mix[l]), w_in[l], tabs, tm=min(seq, 1024))
        oa = _diff_attn(z, batch, seq, row(lambda_q1[l]), row(lambda_k1[l]), row(lambda_q2[l]),
                        row(lambda_k2[l]), row(subln_g[l]), lambda_init, tq=256, tk=512)
        ob_groups = [_band_attn(z, batch, seq, g) for g in range(len(B_GROUPS))]
        x2d = _merge_out(x2d, oa, ob_groups, z, w_proj_a[l], w_proj_b[l], w_out[l], tm=512)
        x2d = _ffn(x2d, row(norm_ffn[l]), w1[l], w2[l], row(norm_final),
                   final_norm=(l == depth - 1), tm=512, tf=1024)
    return x2d.reshape(batch, seq, D_MODEL)


def kernel(x_prompt, x_sample, norm_mix, w_in, lambda_q1, lambda_k1, lambda_q2, lambda_k2, subln_g,
           w_proj_a, w_proj_b, w_out, norm_ffn, w1, w2, norm_final):
    weights = (norm_mix, w_in.astype(BF16), lambda_q1, lambda_k1, lambda_q2, lambda_k2, subln_g,
               w_proj_a.astype(BF16), w_proj_b.astype(BF16), w_out.astype(BF16), norm_ffn,
               w1.astype(BF16), w2.astype(BF16), norm_final)
    return _trunk(x_prompt, *weights), _trunk(x_sample, *weights)
```

```python
import functools
import math

import jax
import jax.numpy as jnp
from jax import lax
from jax.experimental import pallas as pl
from jax.experimental.pallas import tpu as pltpu

D_MODEL = 2048
A_HEADS = 8
A_QK_DIM = 64
A_V_DIM = 2 * A_QK_DIM
B_GROUPS = ((128, 1), (512, 4), (2048, 16))
B_HEADS_PER_GROUP = 4
B_HEAD_DIM = 128
B_OUT_W = B_HEADS_PER_GROUP * B_HEAD_DIM
D_FF = 4 * D_MODEL
ROPE_THETA = 500000.0
ROPE_FRACTION_DEN = 4
NORM_EPS = 1e-6
SUBLN_EPS = 1e-5
NEG_BIG = -1e30
LOG2E = 1.4426950408889634

COL_BLK = 512
IN_WIDTH = 11776
N_COL_BLKS = IN_WIDTH // COL_BLK
QA_BLK, KA_BLK, VA_BLK = 0, 2, 4
QB_BLK, KB_BLK, VB_BLK = 6, 9, 12
GA_BLK, GB_BLK = 15, 19
N_GROUPS = len(B_GROUPS)
LANES = 128
TILE = 1024
BAND_HALF = 64
BAND_SUB = 128
VT_ROWS = A_V_DIM + 16

VMEM_LIMIT = 56 * 1024 * 1024

BF16 = jnp.bfloat16
F32 = jnp.float32


def _params(n_axes):
    return pltpu.CompilerParams(dimension_semantics=("arbitrary",) * n_axes,
                                vmem_limit_bytes=VMEM_LIMIT)


def _sort_rows(a, dilation):
    n, w = a.shape
    return a.reshape(n // TILE, TILE // dilation, dilation, w).transpose(0, 2, 1, 3).reshape(n, w)


def _rope_tables(seq, head_dim):
    rot = head_dim // ROPE_FRACTION_DEN
    pos = jnp.arange(seq, dtype=F32)
    inv = ROPE_THETA ** (-jnp.arange(0, rot, 2, dtype=F32) / rot)
    ang = pos[:, None] * inv[None, :]
    cos, sin = jnp.cos(ang), jnp.sin(ang)
    pad = head_dim - rot
    cos_h = jnp.concatenate([cos, cos, jnp.ones((seq, pad), F32)], axis=1)
    sin_h = jnp.concatenate([-sin, sin, jnp.zeros((seq, pad), F32)], axis=1)
    reps = LANES // head_dim
    return jnp.tile(cos_h, (1, reps)), jnp.tile(sin_h, (1, reps))


def _rope_store(acc, cos, sin, head_dim, scale, z_ref):
    half = head_dim // ROPE_FRACTION_DEN // 2
    lane = lax.broadcasted_iota(jnp.int32, (1, LANES), 1)
    first_half = (lane % head_dim) < half
    for c in range(COL_BLK // LANES):
        xc = acc[:, c * LANES:(c + 1) * LANES]
        up = pltpu.roll(xc, LANES - half, 1)
        dn = pltpu.roll(xc, half, 1)
        out = xc * cos + jnp.where(first_half, up, dn) * sin
        if scale != 1.0:
            out = out * scale
        z_ref[:, c * LANES:(c + 1) * LANES] = out.astype(z_ref.dtype)


def _row_order(j):
    return jnp.where((j >= QB_BLK) & (j < GA_BLK), (j - QB_BLK) % N_GROUPS, 0)


def _in_proj_kernel(x_ref, g_ref, w_ref, ca_ref, sa_ref, cb_ref, sb_ref, z_ref, h_ref, slab_ref):
    j = pl.program_id(1)

    @pl.when(j == 0)
    def _():
        x = x_ref[...]
        rinv = lax.rsqrt(jnp.mean(x * x, axis=-1, keepdims=True) + NORM_EPS)
        h_ref[0] = (x * rinv * g_ref[...]).astype(h_ref.dtype)
        for c in range(D_MODEL // LANES):
            cs = slice(c * LANES, (c + 1) * LANES)
            slab_ref[...] = x_ref[:, cs] * rinv * g_ref[:, cs]
            for order in range(1, N_GROUPS):
                dilation = B_GROUPS[order][1]
                per = TILE // dilation
                for rho in range(dilation):
                    h_ref[order, rho * per:(rho + 1) * per, cs] = (
                        slab_ref[pl.ds(rho, per, stride=dilation), :].astype(h_ref.dtype))

    def project():
        return jnp.dot(h_ref[_row_order(j)], w_ref[...], preferred_element_type=F32)

    @pl.when(j < KA_BLK)
    def _():
        _rope_store(project(), ca_ref[...], sa_ref[...], A_QK_DIM, A_QK_DIM ** -0.5 * LOG2E, z_ref)

    @pl.when((j >= KA_BLK) & (j < VA_BLK))
    def _():
        _rope_store(project(), ca_ref[...], sa_ref[...], A_QK_DIM, 1.0, z_ref)

    @pl.when((j >= QB_BLK) & (j < KB_BLK))
    def _():
        _rope_store(project(), cb_ref[...], sb_ref[...], B_HEAD_DIM, B_HEAD_DIM ** -0.5, z_ref)

    @pl.when((j >= KB_BLK) & (j < VB_BLK))
    def _():
        _rope_store(project(), cb_ref[...], sb_ref[...], B_HEAD_DIM, 1.0, z_ref)

    @pl.when(((j >= VA_BLK) & (j < QB_BLK)) | (j >= VB_BLK))
    def _():
        z_ref[...] = project().astype(z_ref.dtype)


def _in_proj(x2d, seq, g, w_bf16, tabs_a, tabs_b):
    tokens = x2d.shape[0]
    nt = seq // TILE
    tab_a = pl.BlockSpec((TILE, LANES), lambda i, j: (i % nt, 0))
    tab_b = pl.BlockSpec((None, TILE, LANES), lambda i, j: (_row_order(j), i % nt, 0))
    return pl.pallas_call(
        _in_proj_kernel,
        grid=(tokens // TILE, N_COL_BLKS),
        in_specs=[
            pl.BlockSpec((TILE, D_MODEL), lambda i, j: (i, 0)),
            pl.BlockSpec((1, D_MODEL), lambda i, j: (0, 0)),
            pl.BlockSpec((D_MODEL, COL_BLK), lambda i, j: (0, j)),
            tab_a, tab_a, tab_b, tab_b,
        ],
        out_specs=pl.BlockSpec((TILE, COL_BLK), lambda i, j: (i, j)),
        out_shape=jax.ShapeDtypeStruct((tokens, IN_WIDTH), BF16),
        scratch_shapes=[pltpu.VMEM((N_GROUPS, TILE, D_MODEL), BF16), pltpu.VMEM((TILE, LANES), F32)],
        compiler_params=_params(2),
        name="in_proj",
    )(x2d, g, w_bf16, *tabs_a, *tabs_b)


def _diff_attn_kernel(q_ref, k_ref, v_ref, lq1_ref, lk1_ref, lq2_ref, lk2_ref, g_ref, o_ref,
                      q2t_ref, vt_ref, m_ref, acc_ref, s_ref, *, tq, tk, seq, lambda_init):
    qi = pl.program_id(2)
    nck = seq // tk

    @pl.when(qi == 0)
    def _():
        def transpose_chunk(c, carry):
            rows = pl.ds(pl.multiple_of(c * tk, tk), tk)
            vt_ref[c, :A_V_DIM, :] = v_ref[rows, :].astype(F32).T.astype(BF16)
            sub = lax.broadcasted_iota(jnp.int32, (VT_ROWS - A_V_DIM, tk), 0)
            vt_ref[c, A_V_DIM:, :] = jnp.where(sub == 0, 1.0, 0.0).astype(BF16)
            return carry
        lax.fori_loop(0, nck, transpose_chunk, 0)

    qt = q_ref[...].astype(F32).T
    dim = lax.broadcasted_iota(jnp.int32, (A_V_DIM, 1), 0)
    q2t_ref[:, :tq] = jnp.where(dim < A_QK_DIM, qt, 0.0).astype(BF16)
    q2t_ref[:, tq:] = jnp.where(dim >= A_QK_DIM, qt, 0.0).astype(BF16)
    m_ref[...] = jnp.full(m_ref.shape, NEG_BIG, F32)
    acc_ref[...] = jnp.zeros(acc_ref.shape, F32)

    def scores(c):
        rows = pl.ds(pl.multiple_of(c * tk, tk), tk)
        return jnp.dot(k_ref[rows, :], q2t_ref[...], preferred_element_type=F32)

    def accumulate(c, s_cur_ref):
        s = s_cur_ref[...]
        m_old = m_ref[0:1, :]
        m_new = jnp.maximum(m_old, jnp.max(s, axis=0, keepdims=True))
        p = jnp.exp2(s - m_new).astype(BF16)
        pv = jnp.dot(vt_ref[c], p, preferred_element_type=F32)
        acc_ref[...] = jnp.exp2(m_old - m_new) * acc_ref[...] + pv
        m_ref[...] = jnp.broadcast_to(m_new, m_ref.shape)

    s_a, s_b = s_ref.at[0], s_ref.at[1]
    s_a[...] = scores(0)

    def chunk_pair(i, carry):
        s_b[...] = scores(2 * i + 1)
        accumulate(2 * i, s_a)
        s_a[...] = scores(2 * i + 2)
        accumulate(2 * i + 1, s_b)
        return carry

    lax.fori_loop(0, nck // 2 - 1, chunk_pair, 0)
    s_b[...] = scores(nck - 1)
    accumulate(nck - 2, s_a)
    accumulate(nck - 1, s_b)

    acc = acc_ref[...]
    ot = acc[:A_V_DIM] / acc[A_V_DIM:A_V_DIM + 1]
    lam = (jnp.exp(jnp.sum(lq1_ref[...] * lk1_ref[...], axis=-1, keepdims=True))
           - jnp.exp(jnp.sum(lq2_ref[...] * lk2_ref[...], axis=-1, keepdims=True)) + lambda_init)
    o = ot[:, :tq] - lam * ot[:, tq:]
    ms = jnp.mean(o * o, axis=0, keepdims=True)
    o = o * lax.rsqrt(ms + SUBLN_EPS)
    o_ref[...] = (o.T * g_ref[...] * (1.0 - lambda_init)).astype(o_ref.dtype)


def _diff_attn(z, batch, seq, lq1, lk1, lq2, lk2, subln_g, lambda_init, tq, tk):
    tokens = batch * seq
    nq = seq // tq
    assert seq % (2 * tk) == 0
    vec = lambda n: pl.BlockSpec((1, n), lambda b, h, qi: (0, 0))
    kern = functools.partial(_diff_attn_kernel, tq=tq, tk=tk, seq=seq, lambda_init=lambda_init)
    return pl.pallas_call(
        kern,
        grid=(batch, A_HEADS, nq),
        in_specs=[
            pl.BlockSpec((tq, A_V_DIM), lambda b, h, qi: (b * nq + qi, h)),
            pl.BlockSpec((seq, A_V_DIM), lambda b, h, qi: (b, A_HEADS + h)),
            pl.BlockSpec((seq, A_V_DIM), lambda b, h, qi: (b, 2 * A_HEADS + h)),
            vec(A_QK_DIM), vec(A_QK_DIM), vec(A_QK_DIM), vec(A_QK_DIM), vec(A_V_DIM),
        ],
        out_specs=pl.BlockSpec((tq, A_V_DIM), lambda b, h, qi: (b * nq + qi, h)),
        out_shape=jax.ShapeDtypeStruct((tokens, A_HEADS * A_V_DIM), BF16),
        scratch_shapes=[
            pltpu.VMEM((A_V_DIM, 2 * tq), BF16),
            pltpu.VMEM((seq // tk, VT_ROWS, tk), BF16),
            pltpu.VMEM((8, 2 * tq), F32),
            pltpu.VMEM((VT_ROWS, 2 * tq), F32),
            pltpu.VMEM((2, tk, 2 * tq), F32),
        ],
        compiler_params=_params(3),
        name="diff_attn",
    )(z, z, z, lq1, lk1, lq2, lk2, subln_g)


def _band_window(prev_ref, cur_ref, next_ref, base, lo, hi, per, hs):
    parts = []
    if lo < 0:
        parts.append(prev_ref[base + per + lo:base + per + min(hi, 0), hs])
    if hi > 0 and lo < per:
        parts.append(cur_ref[base + max(lo, 0):base + min(hi, per), hs])
    if hi > per:
        parts.append(next_ref[base + max(lo, per) - per:base + hi - per, hs])
    return parts[0] if len(parts) == 1 else jnp.concatenate(parts, axis=0)


def _band_attn_kernel(q_ref, kp_ref, kc_ref, kn_ref, vp_ref, vc_ref, vn_ref, o_ref, lse_ref, of_ref, lf_ref,
                      *, dilation, length):
    t = pl.program_id(1)
    per = TILE // dilation
    sub = min(per, BAND_SUB)
    ncls = BAND_SUB // sub
    nkeys = sub + 2 * BAND_HALF
    shape = (ncls * sub, ncls * nkeys)
    rows = lax.broadcasted_iota(jnp.int32, shape, 0)
    cols = lax.broadcasted_iota(jnp.int32, shape, 1)
    row_cls, row_pos = rows // sub, rows % sub
    col_cls, col_pos = cols // nkeys, cols % nkeys
    in_band = (jnp.abs(col_pos - BAND_HALF - row_pos) <= BAND_HALF) & (row_cls == col_cls)
    for i in range(per // sub):
        kpos = t * per + i * sub - BAND_HALF + col_pos
        mask = in_band & (kpos >= 0) & (kpos < length)
        lo, hi = i * sub - BAND_HALF, i * sub + sub + BAND_HALF
        for rho0 in range(0, dilation, ncls):
            classes = range(rho0, rho0 + ncls)
            for h in range(B_HEADS_PER_GROUP):
                hs = slice(h * B_HEAD_DIM, (h + 1) * B_HEAD_DIM)
                q = q_ref[rho0 * per + i * sub:rho0 * per + i * sub + ncls * sub, hs]
                k = [_band_window(kp_ref, kc_ref, kn_ref, rho * per, lo, hi, per, hs) for rho in classes]
                v = [_band_window(vp_ref, vc_ref, vn_ref, rho * per, lo, hi, per, hs) for rho in classes]
                k = k[0] if ncls == 1 else jnp.concatenate(k, axis=0)
                v = v[0] if ncls == 1 else jnp.concatenate(v, axis=0)
                s = lax.dot_general(q, k, (((1,), (1,)), ((), ())), preferred_element_type=F32)
                s = jnp.where(mask, s, NEG_BIG)
                m = jnp.max(s, axis=-1, keepdims=True)
                p = jnp.exp(s - m)
                l = jnp.sum(p, axis=-1, keepdims=True)
                o = jnp.dot(p.astype(BF16), v, preferred_element_type=F32) / l
                lse = jnp.broadcast_to(m + jnp.log(l), (ncls * sub, B_HEAD_DIM))
                for n, rho in enumerate(classes):
                    out_rows = pl.ds(i * sub * dilation + rho, sub, stride=dilation) if dilation > 1 \
                        else pl.ds(i * sub, sub)
                    of_ref[h, out_rows, :] = o[n * sub:(n + 1) * sub]
                    lf_ref[h, out_rows, :] = lse[n * sub:(n + 1) * sub]
    for h in range(B_HEADS_PER_GROUP):
        hs = slice(h * B_HEAD_DIM, (h + 1) * B_HEAD_DIM)
        o_ref[:, hs] = of_ref[h].astype(o_ref.dtype)
        lse_ref[:, hs] = lf_ref[h]


def _band_attn(z, batch, seq, group):
    dilation = B_GROUPS[group][1]
    assert B_GROUPS[group][0] // (2 * dilation) == BAND_HALF
    nt = seq // TILE

    def tile(blk, shift):
        return pl.BlockSpec((TILE, COL_BLK),
                            lambda b, t: (b * nt + jnp.clip(t + shift, 0, nt - 1), blk + group))

    out_spec = pl.BlockSpec((TILE, B_OUT_W), lambda b, t: (b * nt + t, 0))
    kern = functools.partial(_band_attn_kernel, dilation=dilation, length=seq // dilation)
    return pl.pallas_call(
        kern,
        grid=(batch, nt),
        in_specs=[tile(QB_BLK, 0), tile(KB_BLK, -1), tile(KB_BLK, 0), tile(KB_BLK, 1),
                  tile(VB_BLK, -1), tile(VB_BLK, 0), tile(VB_BLK, 1)],
        out_specs=[out_spec, out_spec],
        out_shape=[jax.ShapeDtypeStruct((batch * seq, B_OUT_W), BF16),
                   jax.ShapeDtypeStruct((batch * seq, B_OUT_W), F32)],
        scratch_shapes=[pltpu.VMEM((B_HEADS_PER_GROUP, TILE, B_HEAD_DIM), F32),
                        pltpu.VMEM((B_HEADS_PER_GROUP, TILE, B_HEAD_DIM), F32)],
        compiler_params=_params(2),
        name=f"band_attn_g{group}",
    )(z, z, z, z, z, z, z)


def _merge_out_kernel(x_ref, oa_ref, o0_ref, o1_ref, o2_ref, l0_ref, l1_ref, l2_ref, ga_ref, gb_ref,
                      wpa_ref, wpb_ref, wout_ref, y_ref, ob_ref):
    c = pl.program_id(1)

    @pl.when(c == 0)
    def _():
        l0, l1, l2 = l0_ref[...], l1_ref[...], l2_ref[...]
        mx = jnp.maximum(jnp.maximum(l0, l1), l2)
        e0, e1, e2 = jnp.exp(l0 - mx), jnp.exp(l1 - mx), jnp.exp(l2 - mx)
        num = (e0 * o0_ref[...].astype(F32) + e1 * o1_ref[...].astype(F32) + e2 * o2_ref[...].astype(F32))
        ob_ref[...] = (num / (e0 + e1 + e2)).astype(ob_ref.dtype)
        y_ref[...] = x_ref[...]

    ya = jnp.dot(oa_ref[...], wpa_ref[...], preferred_element_type=F32)
    yb = jnp.dot(ob_ref[...], wpb_ref[...], preferred_element_type=F32)
    merged = (jax.nn.sigmoid(ga_ref[...].astype(F32)) * ya + jax.nn.sigmoid(gb_ref[...].astype(F32)) * yb)
    y_ref[...] += jnp.dot(merged.astype(BF16), wout_ref[...], preferred_element_type=F32)


def _merge_out(x2d, oa, ob_groups, z, wpa, wpb, wout, tm):
    tokens = x2d.shape[0]
    row = lambda w: pl.BlockSpec((tm, w), lambda i, c: (i, 0))
    n_c = D_MODEL // COL_BLK
    (o0, l0), (o1, l1), (o2, l2) = ob_groups
    return pl.pallas_call(
        _merge_out_kernel,
        grid=(tokens // tm, n_c),
        in_specs=[
            row(D_MODEL), row(A_HEADS * A_V_DIM),
            row(B_OUT_W), row(B_OUT_W), row(B_OUT_W), row(B_OUT_W), row(B_OUT_W), row(B_OUT_W),
            pl.BlockSpec((tm, COL_BLK), lambda i, c: (i, GA_BLK + c)),
            pl.BlockSpec((tm, COL_BLK), lambda i, c: (i, GB_BLK + c)),
            pl.BlockSpec((A_HEADS * A_V_DIM, COL_BLK), lambda i, c: (0, c)),
            pl.BlockSpec((B_OUT_W, COL_BLK), lambda i, c: (0, c)),
            pl.BlockSpec((COL_BLK, D_MODEL), lambda i, c: (c, 0)),
        ],
        out_specs=row(D_MODEL),
        out_shape=jax.ShapeDtypeStruct((tokens, D_MODEL), F32),
        scratch_shapes=[pltpu.VMEM((tm, B_OUT_W), BF16)],
        compiler_params=_params(2),
        name="merge_out",
    )(x2d, oa, o0, o1, o2, l0, l1, l2, z, z, wpa, wpb, wout)


def _ffn_kernel(x_ref, g_ref, w1_ref, w2_ref, gf_ref, y_ref, h_ref, acc_ref, *, final_norm):
    f = pl.program_id(1)

    @pl.when(f == 0)
    def _():
        x = x_ref[...]
        ms = jnp.mean(x * x, axis=-1, keepdims=True)
        h_ref[...] = (x * lax.rsqrt(ms + NORM_EPS) * g_ref[...]).astype(h_ref.dtype)
        acc_ref[...] = x

    u = jnp.dot(h_ref[...], w1_ref[...], preferred_element_type=F32)
    u = jnp.square(jnp.maximum(u, 0.0)).astype(BF16)
    acc_ref[...] += jnp.dot(u, w2_ref[...], preferred_element_type=F32)

    @pl.when(f == pl.num_programs(1) - 1)
    def _():
        y = acc_ref[...]
        if final_norm:
            ms = jnp.mean(y * y, axis=-1, keepdims=True)
            y = y * lax.rsqrt(ms + NORM_EPS) * gf_ref[...]
        y_ref[...] = y


def _ffn(x2d, g, w1, w2, g_final, final_norm, tm, tf):
    tokens = x2d.shape[0]
    kern = functools.partial(_ffn_kernel, final_norm=final_norm)
    return pl.pallas_call(
        kern,
        grid=(tokens // tm, D_FF // tf),
        in_specs=[
            pl.BlockSpec((tm, D_MODEL), lambda i, f: (i, 0)),
            pl.BlockSpec((1, D_MODEL), lambda i, f: (0, 0)),
            pl.BlockSpec((D_MODEL, tf), lambda i, f: (0, f)),
            pl.BlockSpec((tf, D_MODEL), lambda i, f: (f, 0)),
            pl.BlockSpec((1, D_MODEL), lambda i, f: (0, 0)),
        ],
        out_specs=pl.BlockSpec((tm, D_MODEL), lambda i, f: (i, 0)),
        out_shape=jax.ShapeDtypeStruct((tokens, D_MODEL), F32),
        scratch_shapes=[pltpu.VMEM((tm, D_MODEL), BF16), pltpu.VMEM((tm, D_MODEL), F32)],
        compiler_params=_params(2),
        name="ffn",
    )(x2d, g, w1, w2, g_final)


def _trunk(x, norm_mix, w_in, lambda_q1, lambda_k1, lambda_q2, lambda_k2, subln_g,
           w_proj_a, w_proj_b, w_out, norm_ffn, w1, w2, norm_final):
    batch, seq, _ = x.shape
    assert seq % TILE == 0
    depth = w_in.shape[0]
    x2d = x.reshape(batch * seq, D_MODEL)
    tabs_a = _rope_tables(seq, A_QK_DIM)
    tabs_b = tuple(jnp.stack([_sort_rows(t, dilation) for _, dilation in B_GROUPS])
                   for t in _rope_tables(seq, B_HEAD_DIM))
    row = lambda v: v.reshape(1, -1)
    for l in range(depth):
        lambda_init = 0.8 - 0.6 * math.exp(-0.3 * l)
        z = _in_proj(x2d, seq, row(norm_mix[l]), w_in[l], tabs_a, tabs_b)
        oa = _diff_attn(z, batch, seq, row(lambda_q1[l]), row(lambda_k1[l]), row(lambda_q2[l]),
                        row(lambda_k2[l]), row(subln_g[l]), lambda_init, tq=256, tk=1024)
        ob_groups = [_band_attn(z, batch, seq, g) for g in range(N_GROUPS)]
        x2d = _merge_out(x2d, oa, ob_groups, z, w_proj_a[l], w_proj_b[l], w_out[l], tm=512)
        x2d = _ffn(x2d, row(norm_ffn[l]), w1[l], w2[l], row(norm_final),
                   final_norm=(l == depth - 1), tm=512, tf=1024)
    return x2d.reshape(batch, seq, D_MODEL)


def kernel(x_prompt, x_sample, norm_mix, w_in, lambda_q1, lambda_k1, lambda_q2, lambda_k2, subln_g,
           w_proj_a, w_proj_b, w_out, norm_ffn, w1, w2, norm_final):
    weights = (norm_mix, w_in.astype(BF16), lambda_q1, lambda_k1, lambda_q2, lambda_k2, subln_g,
               w_proj_a.astype(BF16), w_proj_b.astype(BF16), w_out.astype(BF16), norm_ffn,
               w1.astype(BF16), w2.astype(BF16), norm_final)
    return _trunk(x_prompt, *weights), _trunk(x_sample, *weights)
```

```python
import functools
import math

import jax
import jax.numpy as jnp
from jax import lax
from jax.experimental import pallas as pl
from jax.experimental.pallas import tpu as pltpu

D_MODEL = 2048
A_HEADS = 8
A_QK_DIM = 64
A_V_DIM = 2 * A_QK_DIM
B_GROUPS = ((128, 1), (512, 4), (2048, 16))
B_HEADS_PER_GROUP = 4
B_HEAD_DIM = 128
B_OUT_W = B_HEADS_PER_GROUP * B_HEAD_DIM
D_FF = 4 * D_MODEL
ROPE_THETA = 500000.0
ROPE_FRACTION_DEN = 4
NORM_EPS = 1e-6
SUBLN_EPS = 1e-5
NEG_BIG = -1e30
LOG2E = 1.4426950408889634

COL_BLK = 512
IN_WIDTH = 11776
N_COL_BLKS = IN_WIDTH // COL_BLK
QA_BLK, KA_BLK, VA_BLK = 0, 2, 4
QB_BLK, KB_BLK, VB_BLK = 6, 9, 12
GA_BLK, GB_BLK = 15, 19
N_GROUPS = len(B_GROUPS)
LANES = 128
TILE = 1024
BAND_HALF = 64
BAND_SUB = 128
VT_ROWS = A_V_DIM + 16

VMEM_LIMIT = 56 * 1024 * 1024

BF16 = jnp.bfloat16
F32 = jnp.float32


def _params(n_axes):
    return pltpu.CompilerParams(dimension_semantics=("arbitrary",) * n_axes,
                                vmem_limit_bytes=VMEM_LIMIT)


def _sort_rows(a, dilation):
    n, w = a.shape
    return a.reshape(n // TILE, TILE // dilation, dilation, w).transpose(0, 2, 1, 3).reshape(n, w)


def _rope_tables(seq, head_dim):
    rot = head_dim // ROPE_FRACTION_DEN
    pos = jnp.arange(seq, dtype=F32)
    inv = ROPE_THETA ** (-jnp.arange(0, rot, 2, dtype=F32) / rot)
    ang = pos[:, None] * inv[None, :]
    cos, sin = jnp.cos(ang), jnp.sin(ang)
    pad = head_dim - rot
    cos_h = jnp.concatenate([cos, cos, jnp.ones((seq, pad), F32)], axis=1)
    sin_h = jnp.concatenate([-sin, sin, jnp.zeros((seq, pad), F32)], axis=1)
    reps = LANES // head_dim
    return jnp.tile(cos_h, (1, reps)), jnp.tile(sin_h, (1, reps))


def _rope_store(acc, cos, sin, head_dim, scale, z_ref):
    half = head_dim // ROPE_FRACTION_DEN // 2
    lane = lax.broadcasted_iota(jnp.int32, (1, LANES), 1)
    first_half = (lane % head_dim) < half
    for c in range(COL_BLK // LANES):
        xc = acc[:, c * LANES:(c + 1) * LANES]
        up = pltpu.roll(xc, LANES - half, 1)
        dn = pltpu.roll(xc, half, 1)
        out = xc * cos + jnp.where(first_half, up, dn) * sin
        if scale != 1.0:
            out = out * scale
        z_ref[:, c * LANES:(c + 1) * LANES] = out.astype(z_ref.dtype)


def _row_order(j):
    return jnp.where((j >= QB_BLK) & (j < GA_BLK), (j - QB_BLK) % N_GROUPS, 0)


def _in_proj_kernel(x_ref, g_ref, w_ref, ca_ref, sa_ref, cb_ref, sb_ref, z_ref, h_ref, slab_ref):
    j = pl.program_id(1)

    @pl.when(j == 0)
    def _():
        x = x_ref[...]
        rinv = lax.rsqrt(jnp.mean(x * x, axis=-1, keepdims=True) + NORM_EPS)
        h_ref[0] = (x * rinv * g_ref[...]).astype(h_ref.dtype)
        for c in range(D_MODEL // LANES):
            cs = slice(c * LANES, (c + 1) * LANES)
            slab_ref[...] = x_ref[:, cs] * rinv * g_ref[:, cs]
            for order in range(1, N_GROUPS):
                dilation = B_GROUPS[order][1]
                per = TILE // dilation
                for rho in range(dilation):
                    h_ref[order, rho * per:(rho + 1) * per, cs] = (
                        slab_ref[pl.ds(rho, per, stride=dilation), :].astype(h_ref.dtype))

    def project():
        return jnp.dot(h_ref[_row_order(j)], w_ref[...], preferred_element_type=F32)

    @pl.when(j < KA_BLK)
    def _():
        _rope_store(project(), ca_ref[...], sa_ref[...], A_QK_DIM, A_QK_DIM ** -0.5 * LOG2E, z_ref)

    @pl.when((j >= KA_BLK) & (j < VA_BLK))
    def _():
        _rope_store(project(), ca_ref[...], sa_ref[...], A_QK_DIM, 1.0, z_ref)

    @pl.when((j >= QB_BLK) & (j < KB_BLK))
    def _():
        _rope_store(project(), cb_ref[...], sb_ref[...], B_HEAD_DIM, B_HEAD_DIM ** -0.5, z_ref)

    @pl.when((j >= KB_BLK) & (j < VB_BLK))
    def _():
        _rope_store(project(), cb_ref[...], sb_ref[...], B_HEAD_DIM, 1.0, z_ref)

    @pl.when(((j >= VA_BLK) & (j < QB_BLK)) | (j >= VB_BLK))
    def _():
        z_ref[...] = project().astype(z_ref.dtype)


def _in_proj(x2d, seq, g, w_bf16, tabs_a, tabs_b):
    tokens = x2d.shape[0]
    nt = seq // TILE
    tab_a = pl.BlockSpec((TILE, LANES), lambda i, j: (i % nt, 0))
    tab_b = pl.BlockSpec((None, TILE, LANES), lambda i, j: (_row_order(j), i % nt, 0))
    return pl.pallas_call(
        _in_proj_kernel,
        grid=(tokens // TILE, N_COL_BLKS),
        in_specs=[
            pl.BlockSpec((TILE, D_MODEL), lambda i, j: (i, 0)),
            pl.BlockSpec((1, D_MODEL), lambda i, j: (0, 0)),
            pl.BlockSpec((D_MODEL, COL_BLK), lambda i, j: (0, j)),
            tab_a, tab_a, tab_b, tab_b,
        ],
        out_specs=pl.BlockSpec((TILE, COL_BLK), lambda i, j: (i, j)),
        out_shape=jax.ShapeDtypeStruct((tokens, IN_WIDTH), BF16),
        scratch_shapes=[pltpu.VMEM((N_GROUPS, TILE, D_MODEL), BF16), pltpu.VMEM((TILE, LANES), F32)],
        compiler_params=_params(2),
        name="in_proj",
    )(x2d, g, w_bf16, *tabs_a, *tabs_b)


def _diff_attn_kernel(q_ref, k_ref, v_ref, lq1_ref, lk1_ref, lq2_ref, lk2_ref, g_ref, o_ref,
                      q2t_ref, vt_ref, m_ref, acc_ref, s_ref, *, tq, tk, seq, lambda_init):
    qi = pl.program_id(2)
    nck = seq // tk

    @pl.when(qi == 0)
    def _():
        def transpose_chunk(c, carry):
            rows = pl.ds(pl.multiple_of(c * tk, tk), tk)
            vt_ref[c, :A_V_DIM, :] = v_ref[rows, :].astype(F32).T.astype(BF16)
            sub = lax.broadcasted_iota(jnp.int32, (VT_ROWS - A_V_DIM, tk), 0)
            vt_ref[c, A_V_DIM:, :] = jnp.where(sub == 0, 1.0, 0.0).astype(BF16)
            return carry
        lax.fori_loop(0, nck, transpose_chunk, 0)

    qt = q_ref[...].astype(F32).T
    dim = lax.broadcasted_iota(jnp.int32, (A_V_DIM, 1), 0)
    q2t_ref[:, :tq] = jnp.where(dim < A_QK_DIM, qt, 0.0).astype(BF16)
    q2t_ref[:, tq:] = jnp.where(dim >= A_QK_DIM, qt, 0.0).astype(BF16)
    m_ref[...] = jnp.full(m_ref.shape, NEG_BIG, F32)
    acc_ref[...] = jnp.zeros(acc_ref.shape, F32)

    def scores(c):
        rows = pl.ds(pl.multiple_of(c * tk, tk), tk)
        return jnp.dot(k_ref[rows, :], q2t_ref[...], preferred_element_type=F32)

    def accumulate(c, s_cur_ref):
        s = s_cur_ref[...]
        m_old = m_ref[0:1, :]
        m_new = jnp.maximum(m_old, jnp.max(s, axis=0, keepdims=True))
        p = jnp.exp2(s - m_new).astype(BF16)
        pv = jnp.dot(vt_ref[c], p, preferred_element_type=F32)
        acc_ref[...] = jnp.exp2(m_old - m_new) * acc_ref[...] + pv
        m_ref[...] = jnp.broadcast_to(m_new, m_ref.shape)

    s_a, s_b = s_ref.at[0], s_ref.at[1]
    s_a[...] = scores(0)

    def chunk_pair(i, carry):
        s_b[...] = scores(2 * i + 1)
        accumulate(2 * i, s_a)
        s_a[...] = scores(2 * i + 2)
        accumulate(2 * i + 1, s_b)
        return carry

    lax.fori_loop(0, nck // 2 - 1, chunk_pair, 0)
    s_b[...] = scores(nck - 1)
    accumulate(nck - 2, s_a)
    accumulate(nck - 1, s_b)

    acc = acc_ref[...]
    ot = acc[:A_V_DIM] / acc[A_V_DIM:A_V_DIM + 1]
    lam = (jnp.exp(jnp.sum(lq1_ref[...] * lk1_ref[...], axis=-1, keepdims=True))
           - jnp.exp(jnp.sum(lq2_ref[...] * lk2_ref[...], axis=-1, keepdims=True)) + lambda_init)
    o = ot[:, :tq] - lam * ot[:, tq:]
    ms = jnp.mean(o * o, axis=0, keepdims=True)
    o = o * lax.rsqrt(ms + SUBLN_EPS)
    o_ref[...] = (o.T * g_ref[...] * (1.0 - lambda_init)).astype(o_ref.dtype)


def _diff_attn(z, batch, seq, lq1, lk1, lq2, lk2, subln_g, lambda_init, tq, tk):
    tokens = batch * seq
    nq = seq // tq
    assert seq % (2 * tk) == 0
    vec = lambda n: pl.BlockSpec((1, n), lambda b, h, qi: (0, 0))
    kern = functools.partial(_diff_attn_kernel, tq=tq, tk=tk, seq=seq, lambda_init=lambda_init)
    return pl.pallas_call(
        kern,
        grid=(batch, A_HEADS, nq),
        in_specs=[
            pl.BlockSpec((tq, A_V_DIM), lambda b, h, qi: (b * nq + qi, h)),
            pl.BlockSpec((seq, A_V_DIM), lambda b, h, qi: (b, A_HEADS + h)),
            pl.BlockSpec((seq, A_V_DIM), lambda b, h, qi: (b, 2 * A_HEADS + h)),
            vec(A_QK_DIM), vec(A_QK_DIM), vec(A_QK_DIM), vec(A_QK_DIM), vec(A_V_DIM),
        ],
        out_specs=pl.BlockSpec((tq, A_V_DIM), lambda b, h, qi: (b * nq + qi, h)),
        out_shape=jax.ShapeDtypeStruct((tokens, A_HEADS * A_V_DIM), BF16),
        scratch_shapes=[
            pltpu.VMEM((A_V_DIM, 2 * tq), BF16),
            pltpu.VMEM((seq // tk, VT_ROWS, tk), BF16),
            pltpu.VMEM((8, 2 * tq), F32),
            pltpu.VMEM((VT_ROWS, 2 * tq), F32),
            pltpu.VMEM((2, tk, 2 * tq), F32),
        ],
        compiler_params=_params(3),
        name="diff_attn",
    )(z, z, z, lq1, lk1, lq2, lk2, subln_g)


def _band_window(prev_ref, cur_ref, next_ref, base, lo, hi, per, hs):
    parts = []
    if lo < 0:
        parts.append(prev_ref[base + per + lo:base + per + min(hi, 0), hs])
    if hi > 0 and lo < per:
        parts.append(cur_ref[base + max(lo, 0):base + min(hi, per), hs])
    if hi > per:
        parts.append(next_ref[base + max(lo, per) - per:base + hi - per, hs])
    return parts[0] if len(parts) == 1 else jnp.concatenate(parts, axis=0)


def _band_attn_kernel(q_ref, kp_ref, kc_ref, kn_ref, vp_ref, vc_ref, vn_ref, o_ref, lse_ref, of_ref, lf_ref,
                      *, dilation, length):
    t = pl.program_id(1)
    per = TILE // dilation
    sub = min(per, BAND_SUB)
    ncls = BAND_SUB // sub
    nkeys = sub + 2 * BAND_HALF
    shape = (ncls * sub, ncls * nkeys)
    rows = lax.broadcasted_iota(jnp.int32, shape, 0)
    cols = lax.broadcasted_iota(jnp.int32, shape, 1)
    row_cls, row_pos = rows // sub, rows % sub
    col_cls, col_pos = cols // nkeys, cols % nkeys
    in_band = (jnp.abs(col_pos - BAND_HALF - row_pos) <= BAND_HALF) & (row_cls == col_cls)
    for i in range(per // sub):
        kpos = t * per + i * sub - BAND_HALF + col_pos
        mask = in_band & (kpos >= 0) & (kpos < length)
        lo, hi = i * sub - BAND_HALF, i * sub + sub + BAND_HALF
        for rho0 in range(0, dilation, ncls):
            classes = range(rho0, rho0 + ncls)
            for h in range(B_HEADS_PER_GROUP):
                hs = slice(h * B_HEAD_DIM, (h + 1) * B_HEAD_DIM)
                q = q_ref[rho0 * per + i * sub:rho0 * per + i * sub + ncls * sub, hs]
                k = [_band_window(kp_ref, kc_ref, kn_ref, rho * per, lo, hi, per, hs) for rho in classes]
                v = [_band_window(vp_ref, vc_ref, vn_ref, rho * per, lo, hi, per, hs) for rho in classes]
                k = k[0] if ncls == 1 else jnp.concatenate(k, axis=0)
                v = v[0] if ncls == 1 else jnp.concatenate(v, axis=0)
                s = lax.dot_general(q, k, (((1,), (1,)), ((), ())), preferred_element_type=F32)
                s = jnp.where(mask, s, NEG_BIG)
                m = jnp.max(s, axis=-1, keepdims=True)
                p = jnp.exp(s - m)
                l = jnp.sum(p, axis=-1, keepdims=True)
                o = jnp.dot(p.astype(BF16), v, preferred_element_type=F32) / l
                lse = jnp.broadcast_to(m + jnp.log(l), (ncls * sub, B_HEAD_DIM))
                for n, rho in enumerate(classes):
                    out_rows = pl.ds(i * sub * dilation + rho, sub, stride=dilation) if dilation > 1 \
                        else pl.ds(i * sub, sub)
                    of_ref[h, out_rows, :] = o[n * sub:(n + 1) * sub]
                    lf_ref[h, out_rows, :] = lse[n * sub:(n + 1) * sub]
    for h in range(B_HEADS_PER_GROUP):
        hs = slice(h * B_HEAD_DIM, (h + 1) * B_HEAD_DIM)
        o_ref[:, hs] = of_ref[h].astype(o_ref.dtype)
        lse_ref[:, hs] = lf_ref[h]


def _band_attn(z, batch, seq, group):
    dilation = B_GROUPS[group][1]
    assert B_GROUPS[group][0] // (2 * dilation) == BAND_HALF
    nt = seq // TILE

    def tile(blk, shift):
        return pl.BlockSpec((TILE, COL_BLK),
                            lambda b, t: (b * nt + jnp.clip(t + shift, 0, nt - 1), blk + group))

    out_spec = pl.BlockSpec((TILE, B_OUT_W), lambda b, t: (b * nt + t, 0))
    kern = functools.partial(_band_attn_kernel, dilation=dilation, length=seq // dilation)
    return pl.pallas_call(
        kern,
        grid=(batch, nt),
        in_specs=[tile(QB_BLK, 0), tile(KB_BLK, -1), tile(KB_BLK, 0), tile(KB_BLK, 1),
                  tile(VB_BLK, -1), tile(VB_BLK, 0), tile(VB_BLK, 1)],
        out_specs=[out_spec, out_spec],
        out_shape=[jax.ShapeDtypeStruct((batch * seq, B_OUT_W), BF16),
                   jax.ShapeDtypeStruct((batch * seq, B_OUT_W), F32)],
        scratch_shapes=[pltpu.VMEM((B_HEADS_PER_GROUP, TILE, B_HEAD_DIM), F32),
                        pltpu.VMEM((B_HEADS_PER_GROUP, TILE, B_HEAD_DIM), F32)],
        compiler_params=_params(2),
        name=f"band_attn_g{group}",
    )(z, z, z, z, z, z, z)


def _merge_out_kernel(x_ref, oa_ref, o0_ref, o1_ref, o2_ref, l0_ref, l1_ref, l2_ref, *rest):
    n_c = D_MODEL // COL_BLK
    ga_refs, gb_refs = rest[:n_c], rest[n_c:2 * n_c]
    wpa_ref, wpb_ref, wout_ref, y_ref = rest[2 * n_c:]
    l0, l1, l2 = l0_ref[...], l1_ref[...], l2_ref[...]
    mx = jnp.maximum(jnp.maximum(l0, l1), l2)
    e0, e1, e2 = jnp.exp(l0 - mx), jnp.exp(l1 - mx), jnp.exp(l2 - mx)
    num = (e0 * o0_ref[...].astype(F32) + e1 * o1_ref[...].astype(F32) + e2 * o2_ref[...].astype(F32))
    ob = (num / (e0 + e1 + e2)).astype(BF16)
    ya = jnp.dot(oa_ref[...], wpa_ref[...], preferred_element_type=F32)
    yb = jnp.dot(ob, wpb_ref[...], preferred_element_type=F32)
    merged = []
    for c in range(n_c):
        cs = slice(c * COL_BLK, (c + 1) * COL_BLK)
        merged.append((jax.nn.sigmoid(ga_refs[c][...].astype(F32)) * ya[:, cs]
                       + jax.nn.sigmoid(gb_refs[c][...].astype(F32)) * yb[:, cs]).astype(BF16))
    merged = jnp.concatenate(merged, axis=1)
    y_ref[...] = x_ref[...] + jnp.dot(merged, wout_ref[...], preferred_element_type=F32)


def _merge_out(x2d, oa, ob_groups, z, wpa, wpb, wout, tm):
    tokens = x2d.shape[0]
    row = lambda w: pl.BlockSpec((tm, w), lambda i: (i, 0))
    gate = lambda blk: pl.BlockSpec((tm, COL_BLK), lambda i: (i, blk))
    whole = lambda a: pl.BlockSpec(a.shape, lambda i: (0, 0), pipeline_mode=pl.Buffered(1))
    n_c = D_MODEL // COL_BLK
    (o0, l0), (o1, l1), (o2, l2) = ob_groups
    return pl.pallas_call(
        _merge_out_kernel,
        grid=(tokens // tm,),
        in_specs=[row(D_MODEL), row(A_HEADS * A_V_DIM)] + [row(B_OUT_W)] * 6
        + [gate(GA_BLK + c) for c in range(n_c)] + [gate(GB_BLK + c) for c in range(n_c)]
        + [whole(wpa), whole(wpb), whole(wout)],
        out_specs=row(D_MODEL),
        out_shape=jax.ShapeDtypeStruct((tokens, D_MODEL), F32),
        compiler_params=_params(1),
        name="merge_out",
    )(x2d, oa, o0, o1, o2, l0, l1, l2, *([z] * (2 * n_c)), wpa, wpb, wout)


def _ffn_kernel(x_ref, g_ref, w1_ref, w2_ref, gf_ref, y_ref, h_ref, acc_ref, *, final_norm):
    f = pl.program_id(1)

    @pl.when(f == 0)
    def _():
        x = x_ref[...]
        ms = jnp.mean(x * x, axis=-1, keepdims=True)
        h_ref[...] = (x * lax.rsqrt(ms + NORM_EPS) * g_ref[...]).astype(h_ref.dtype)
        acc_ref[...] = x

    u = jnp.dot(h_ref[...], w1_ref[...], preferred_element_type=F32)
    u = jnp.square(jnp.maximum(u, 0.0)).astype(BF16)
    acc_ref[...] += jnp.dot(u, w2_ref[...], preferred_element_type=F32)

    @pl.when(f == pl.num_programs(1) - 1)
    def _():
        y = acc_ref[...]
        if final_norm:
            ms = jnp.mean(y * y, axis=-1, keepdims=True)
            y = y * lax.rsqrt(ms + NORM_EPS) * gf_ref[...]
        y_ref[...] = y


def _ffn(x2d, g, w1, w2, g_final, final_norm, tm, tf):
    tokens = x2d.shape[0]
    kern = functools.partial(_ffn_kernel, final_norm=final_norm)
    return pl.pallas_call(
        kern,
        grid=(tokens // tm, D_FF // tf),
        in_specs=[
            pl.BlockSpec((tm, D_MODEL), lambda i, f: (i, 0)),
            pl.BlockSpec((1, D_MODEL), lambda i, f: (0, 0)),
            pl.BlockSpec((D_MODEL, tf), lambda i, f: (0, f)),
            pl.BlockSpec((tf, D_MODEL), lambda i, f: (f, 0)),
            pl.BlockSpec((1, D_MODEL), lambda i, f: (0, 0)),
        ],
        out_specs=pl.BlockSpec((tm, D_MODEL), lambda i, f: (i, 0)),
        out_shape=jax.ShapeDtypeStruct((tokens, D_MODEL), F32),
        scratch_shapes=[pltpu.VMEM((tm, D_MODEL), BF16), pltpu.VMEM((tm, D_MODEL), F32)],
        compiler_params=_params(2),
        name="ffn",
    )(x2d, g, w1, w2, g_final)


def _trunk(x, norm_mix, w_in, lambda_q1, lambda_k1, lambda_q2, lambda_k2, subln_g,
           w_proj_a, w_proj_b, w_out, norm_ffn, w1, w2, norm_final):
    batch, seq, _ = x.shape
    assert seq % TILE == 0
    depth = w_in.shape[0]
    x2d = x.reshape(batch * seq, D_MODEL)
    tabs_a = _rope_tables(seq, A_QK_DIM)
    tabs_b = tuple(jnp.stack([_sort_rows(t, dilation) for _, dilation in B_GROUPS])
                   for t in _rope_tables(seq, B_HEAD_DIM))
    row = lambda v: v.reshape(1, -1)
    for l in range(depth):
        lambda_init = 0.8 - 0.6 * math.exp(-0.3 * l)
        z = _in_proj(x2d, seq, row(norm_mix[l]), w_in[l], tabs_a, tabs_b)
        oa = _diff_attn(z, batch, seq, row(lambda_q1[l]), row(lambda_k1[l]), row(lambda_q2[l]),
                        row(lambda_k2[l]), row(subln_g[l]), lambda_init, tq=512, tk=1024)
        ob_groups = [_band_attn(z, batch, seq, g) for g in range(N_GROUPS)]
        x2d = _merge_out(x2d, oa, ob_groups, z, w_proj_a[l], w_proj_b[l], w_out[l], tm=256)
        x2d = _ffn(x2d, row(norm_ffn[l]), w1[l], w2[l], row(norm_final),
                   final_norm=(l == depth - 1), tm=512, tf=1024)
    return x2d.reshape(batch, seq, D_MODEL)


def kernel(x_prompt, x_sample, norm_mix, w_in, lambda_q1, lambda_k1, lambda_q2, lambda_k2, subln_g,
           w_proj_a, w_proj_b, w_out, norm_ffn, w1, w2, norm_final):
    weights = (norm_mix, w_in.astype(BF16), lambda_q1, lambda_k1, lambda_q2, lambda_k2, subln_g,
               w_proj_a.astype(BF16), w_proj_b.astype(BF16), w_out.astype(BF16), norm_ffn,
               w1.astype(BF16), w2.astype(BF16), norm_final)
    return _trunk(x_prompt, *weights), _trunk(x_sample, *weights)
```

```python
import functools
import math

import jax
import jax.numpy as jnp
import numpy as np
from jax import lax
from jax.experimental import pallas as pl
from jax.experimental.pallas import tpu as pltpu

D_MODEL = 2048
A_HEADS = 8
A_QK_DIM = 64
A_V_DIM = 2 * A_QK_DIM
B_GROUPS = ((128, 1), (512, 4), (2048, 16))
B_HEADS_PER_GROUP = 4
B_HEAD_DIM = 128
B_OUT_W = B_HEADS_PER_GROUP * B_HEAD_DIM
D_FF = 4 * D_MODEL
ROPE_THETA = 500000.0
ROPE_FRACTION_DEN = 4
NORM_EPS = 1e-6
SUBLN_EPS = 1e-5
NEG_BIG = -1e30
LOG2E = 1.4426950408889634

COL_BLK = 512
IN_WIDTH = 11776
N_COL_BLKS = IN_WIDTH // COL_BLK
QA_BLK, KA_BLK, VA_BLK = 0, 2, 4
QB_BLK, KB_BLK, VB_BLK = 6, 9, 12
GA_BLK, GB_BLK = 15, 19
N_GROUPS = len(B_GROUPS)
LANES = 128
TILE = 1024
BAND_HALF = 64
BAND_SUB = 128
BAND_CHAINS = (1, 1, 16)
VT_ROWS = A_V_DIM + 16

VMEM_LIMIT = 56 * 1024 * 1024

BF16 = jnp.bfloat16
F32 = jnp.float32


def _params(n_axes):
    return pltpu.CompilerParams(dimension_semantics=("arbitrary",) * n_axes,
                                vmem_limit_bytes=VMEM_LIMIT)


def _sort_rows(a, dilation):
    n, w = a.shape
    return a.reshape(n // TILE, TILE // dilation, dilation, w).transpose(0, 2, 1, 3).reshape(n, w)


def _rotary_layout(head_dim):
    half = head_dim // ROPE_FRACTION_DEN // 2
    heads = LANES // head_dim
    first = [h * head_dim + i for h in range(heads) for i in range(half)]
    second = [h * head_dim + half + i for h in range(heads) for i in range(half)]
    target1 = list(range(len(first)))
    target2 = list(range(LANES // 2, LANES // 2 + len(second)))
    src = np.arange(LANES)
    freq = np.full(LANES, -1)
    sign = np.zeros(LANES)
    for lanes, cols, sgn in ((target1, first, -1.0), (target2, second, 1.0)):
        for j, (lane, col) in enumerate(zip(lanes, cols)):
            src[lane], freq[lane], sign[lane] = col, j % half, sgn
    rotary, targets = set(first + second), set(target1 + target2)
    for lane, col in zip(sorted(rotary - targets), sorted(targets - rotary)):
        src[lane] = col
    assert sorted(src) == list(range(LANES))
    return src, freq, sign


def _runs(idx):
    out, start = [], 0
    for i in range(1, len(idx) + 1):
        if i == len(idx) or idx[i] != idx[i - 1] + 1:
            out.append((int(idx[start]), int(idx[i - 1]) + 1))
            start = i
    return out


def _permute_rotary_columns(w):
    def section(lo, hi, head_dim):
        ws = w[..., lo:hi].reshape(w.shape[:-1] + ((hi - lo) // LANES, LANES))
        src = _rotary_layout(head_dim)[0]
        ws = jnp.concatenate([ws[..., a:b] for a, b in _runs(src)], axis=-1)
        return ws.reshape(w.shape[:-1] + (hi - lo,))
    a_end, b_lo, b_end = VA_BLK * COL_BLK, QB_BLK * COL_BLK, VB_BLK * COL_BLK
    return jnp.concatenate([section(0, a_end, A_QK_DIM), w[..., a_end:b_lo],
                            section(b_lo, b_end, B_HEAD_DIM), w[..., b_end:]], axis=-1)


def _rope_tables(seq, head_dim):
    rot = head_dim // ROPE_FRACTION_DEN
    pos = jnp.arange(seq, dtype=F32)
    inv = ROPE_THETA ** (-jnp.arange(0, rot, 2, dtype=F32) / rot)
    ang = pos[:, None] * inv[None, :]
    cos, sin = jnp.cos(ang), jnp.sin(ang)
    _, freq, sign = _rotary_layout(head_dim)
    cos_parts, sin_parts, lane = [], [], 0
    while lane < LANES:
        end = lane + 1
        if freq[lane] < 0:
            while end < LANES and freq[end] < 0:
                end += 1
            cos_parts.append(jnp.ones((seq, end - lane), F32))
            sin_parts.append(jnp.zeros((seq, end - lane), F32))
        else:
            while end < LANES and freq[end] == freq[end - 1] + 1:
                end += 1
            cols = slice(int(freq[lane]), int(freq[end - 1]) + 1)
            cos_parts.append(cos[:, cols])
            sin_parts.append(sin[:, cols] * float(sign[lane]))
        lane = end
    return jnp.concatenate(cos_parts, axis=1), jnp.concatenate(sin_parts, axis=1)


def _rope_store(acc, cos, sin, scale, z_ref):
    for c in range(COL_BLK // LANES):
        xc = acc[:, c * LANES:(c + 1) * LANES]
        out = xc * cos + pltpu.roll(xc, LANES // 2, 1) * sin
        if scale != 1.0:
            out = out * scale
        z_ref[:, c * LANES:(c + 1) * LANES] = out.astype(z_ref.dtype)


def _row_order(j):
    return jnp.where((j >= QB_BLK) & (j < GA_BLK), (j - QB_BLK) % N_GROUPS, 0)


def _in_proj_kernel(x_ref, g_ref, w_ref, ca_ref, sa_ref, cb_ref, sb_ref, z_ref, h_ref, slab_ref):
    j = pl.program_id(1)

    @pl.when(j == 0)
    def _():
        x = x_ref[...]
        rinv = lax.rsqrt(jnp.mean(x * x, axis=-1, keepdims=True) + NORM_EPS)
        h_ref[0] = (x * rinv * g_ref[...]).astype(h_ref.dtype)
        for c in range(D_MODEL // LANES):
            cs = slice(c * LANES, (c + 1) * LANES)
            slab_ref[...] = x_ref[:, cs] * rinv * g_ref[:, cs]
            for order in range(1, N_GROUPS):
                dilation = B_GROUPS[order][1]
                per = TILE // dilation
                for rho in range(dilation):
                    h_ref[order, rho * per:(rho + 1) * per, cs] = (
                        slab_ref[pl.ds(rho, per, stride=dilation), :].astype(h_ref.dtype))

    def project():
        return jnp.dot(h_ref[_row_order(j)], w_ref[...], preferred_element_type=F32)

    @pl.when(j < KA_BLK)
    def _():
        _rope_store(project(), ca_ref[...], sa_ref[...], A_QK_DIM ** -0.5 * LOG2E, z_ref)

    @pl.when((j >= KA_BLK) & (j < VA_BLK))
    def _():
        _rope_store(project(), ca_ref[...], sa_ref[...], 1.0, z_ref)

    @pl.when((j >= QB_BLK) & (j < KB_BLK))
    def _():
        _rope_store(project(), cb_ref[...], sb_ref[...], B_HEAD_DIM ** -0.5, z_ref)

    @pl.when((j >= KB_BLK) & (j < VB_BLK))
    def _():
        _rope_store(project(), cb_ref[...], sb_ref[...], 1.0, z_ref)

    @pl.when(((j >= VA_BLK) & (j < QB_BLK)) | (j >= VB_BLK))
    def _():
        z_ref[...] = project().astype(z_ref.dtype)


def _in_proj(x2d, seq, g, w_bf16, tabs_a, tabs_b):
    tokens = x2d.shape[0]
    nt = seq // TILE
    tab_a = pl.BlockSpec((TILE, LANES), lambda i, j: (i % nt, 0))
    tab_b = pl.BlockSpec((None, TILE, LANES), lambda i, j: (_row_order(j), i % nt, 0))
    return pl.pallas_call(
        _in_proj_kernel,
        grid=(tokens // TILE, N_COL_BLKS),
        in_specs=[
            pl.BlockSpec((TILE, D_MODEL), lambda i, j: (i, 0)),
            pl.BlockSpec((1, D_MODEL), lambda i, j: (0, 0)),
            pl.BlockSpec((D_MODEL, COL_BLK), lambda i, j: (0, j)),
            tab_a, tab_a, tab_b, tab_b,
        ],
        out_specs=pl.BlockSpec((TILE, COL_BLK), lambda i, j: (i, j)),
        out_shape=jax.ShapeDtypeStruct((tokens, IN_WIDTH), BF16),
        scratch_shapes=[pltpu.VMEM((N_GROUPS, TILE, D_MODEL), BF16), pltpu.VMEM((TILE, LANES), F32)],
        compiler_params=_params(2),
        name="in_proj",
    )(x2d, g, w_bf16, *tabs_a, *tabs_b)


_MAP0_LANES = _runs(np.flatnonzero(_rotary_layout(A_QK_DIM)[0] < A_QK_DIM))


def _diff_attn_kernel(q_ref, k_ref, v_ref, lq1_ref, lk1_ref, lq2_ref, lk2_ref, g_ref, o_ref,
                      q2t_ref, vt_ref, m_ref, acc_ref, s_ref, *, tq, tk, seq, lambda_init):
    qi = pl.program_id(2)
    nck = seq // tk

    @pl.when(qi == 0)
    def _():
        def transpose_chunk(c, carry):
            rows = pl.ds(pl.multiple_of(c * tk, tk), tk)
            vt_ref[c, :A_V_DIM, :] = v_ref[rows, :].astype(F32).T.astype(BF16)
            sub = lax.broadcasted_iota(jnp.int32, (VT_ROWS - A_V_DIM, tk), 0)
            vt_ref[c, A_V_DIM:, :] = jnp.where(sub == 0, 1.0, 0.0).astype(BF16)
            return carry
        lax.fori_loop(0, nck, transpose_chunk, 0)

    qt = q_ref[...].astype(F32).T
    dim = lax.broadcasted_iota(jnp.int32, (A_V_DIM, 1), 0)
    map0 = functools.reduce(jnp.logical_or, [(dim >= a) & (dim < b) for a, b in _MAP0_LANES])
    q2t_ref[:, :tq] = jnp.where(map0, qt, 0.0).astype(BF16)
    q2t_ref[:, tq:] = jnp.where(map0, 0.0, qt).astype(BF16)
    m_ref[...] = jnp.full(m_ref.shape, NEG_BIG, F32)
    acc_ref[...] = jnp.zeros(acc_ref.shape, F32)

    def scores(c):
        rows = pl.ds(pl.multiple_of(c * tk, tk), tk)
        return jnp.dot(k_ref[rows, :], q2t_ref[...], preferred_element_type=F32)

    def accumulate(c, s_cur_ref):
        s = s_cur_ref[...]
        m_old = m_ref[0:1, :]
        m_new = jnp.maximum(m_old, jnp.max(s, axis=0, keepdims=True))
        p = jnp.exp2(s - m_new).astype(BF16)
        pv = jnp.dot(vt_ref[c], p, preferred_element_type=F32)
        acc_ref[...] = jnp.exp2(m_old - m_new) * acc_ref[...] + pv
        m_ref[...] = jnp.broadcast_to(m_new, m_ref.shape)

    s_a, s_b = s_ref.at[0], s_ref.at[1]
    s_a[...] = scores(0)

    bufs = (s_a, s_b)
    group = 4 if nck % 4 == 0 else 2

    def trip(c0, last):
        for u in range(group):
            if not (last and u == group - 1):
                bufs[(u + 1) % 2][...] = scores(c0 + u + 1)
            accumulate(c0 + u, bufs[u % 2])

    def body(i, carry):
        trip(group * i, False)
        return carry

    lax.fori_loop(0, nck // group - 1, body, 0)
    trip(nck - group, True)

    acc = acc_ref[...]
    ot = acc[:A_V_DIM] / acc[A_V_DIM:A_V_DIM + 1]
    lam = (jnp.exp(jnp.sum(lq1_ref[...] * lk1_ref[...], axis=-1, keepdims=True))
           - jnp.exp(jnp.sum(lq2_ref[...] * lk2_ref[...], axis=-1, keepdims=True)) + lambda_init)
    o = ot[:, :tq] - lam * ot[:, tq:]
    ms = jnp.mean(o * o, axis=0, keepdims=True)
    o = o * lax.rsqrt(ms + SUBLN_EPS)
    o_ref[...] = (o.T * g_ref[...] * (1.0 - lambda_init)).astype(o_ref.dtype)


def _diff_attn(z, batch, seq, lq1, lk1, lq2, lk2, subln_g, lambda_init, tq, tk):
    tokens = batch * seq
    nq = seq // tq
    assert seq % (2 * tk) == 0
    vec = lambda n: pl.BlockSpec((1, n), lambda b, h, qi: (0, 0))
    kern = functools.partial(_diff_attn_kernel, tq=tq, tk=tk, seq=seq, lambda_init=lambda_init)
    return pl.pallas_call(
        kern,
        grid=(batch, A_HEADS, nq),
        in_specs=[
            pl.BlockSpec((tq, A_V_DIM), lambda b, h, qi: (b * nq + qi, h)),
            pl.BlockSpec((seq, A_V_DIM), lambda b, h, qi: (b, A_HEADS + h)),
            pl.BlockSpec((seq, A_V_DIM), lambda b, h, qi: (b, 2 * A_HEADS + h)),
            vec(A_QK_DIM), vec(A_QK_DIM), vec(A_QK_DIM), vec(A_QK_DIM), vec(A_V_DIM),
        ],
        out_specs=pl.BlockSpec((tq, A_V_DIM), lambda b, h, qi: (b * nq + qi, h)),
        out_shape=jax.ShapeDtypeStruct((tokens, A_HEADS * A_V_DIM), BF16),
        scratch_shapes=[
            pltpu.VMEM((A_V_DIM, 2 * tq), BF16),
            pltpu.VMEM((seq // tk, VT_ROWS, tk), BF16),
            pltpu.VMEM((8, 2 * tq), F32),
            pltpu.VMEM((VT_ROWS, 2 * tq), F32),
            pltpu.VMEM((2, tk, 2 * tq), F32),
        ],
        compiler_params=_params(3),
        name="diff_attn",
    )(z, z, z, lq1, lk1, lq2, lk2, subln_g)


def _band_window(prev_ref, cur_ref, next_ref, base, lo, hi, per, hs):
    parts = []
    if lo < 0:
        parts.append(prev_ref[base + per + lo:base + per + min(hi, 0), hs])
    if hi > 0 and lo < per:
        parts.append(cur_ref[base + max(lo, 0):base + min(hi, per), hs])
    if hi > per:
        parts.append(next_ref[base + max(lo, per) - per:base + hi - per, hs])
    return parts[0] if len(parts) == 1 else jnp.concatenate(parts, axis=0)


def _band_attn_kernel(q_ref, kp_ref, kc_ref, kn_ref, vp_ref, vc_ref, vn_ref, o_ref, lse_ref, of_ref, lf_ref,
                      *, dilation, length, chains):
    t = pl.program_id(1)
    per = TILE // dilation
    sub = min(per, BAND_SUB)
    ncls = BAND_SUB // sub
    nkeys = sub + 2 * BAND_HALF
    shape = (ncls * sub, ncls * nkeys)
    rows = lax.broadcasted_iota(jnp.int32, shape, 0)
    cols = lax.broadcasted_iota(jnp.int32, shape, 1)
    row_cls, row_pos = rows // sub, rows % sub
    col_cls, col_pos = cols // nkeys, cols % nkeys
    in_band = (jnp.abs(col_pos - BAND_HALF - row_pos) <= BAND_HALF) & (row_cls == col_cls)

    def window(refs, classes, lo, hi, hs):
        parts = [_band_window(*refs, rho * per, lo, hi, per, hs) for rho in classes]
        return parts[0] if len(parts) == 1 else jnp.concatenate(parts, axis=0)

    for i in range(per // sub):
        kpos = t * per + i * sub - BAND_HALF + col_pos
        mask = in_band & (kpos >= 0) & (kpos < length)
        lo, hi = i * sub - BAND_HALF, i * sub + sub + BAND_HALF
        tiles = [(range(rho0, rho0 + ncls), h) for rho0 in range(0, dilation, ncls)
                 for h in range(B_HEADS_PER_GROUP)]
        for b0 in range(0, len(tiles), chains):
            batch = tiles[b0:b0 + chains]
            scores = []
            for classes, h in batch:
                hs = slice(h * B_HEAD_DIM, (h + 1) * B_HEAD_DIM)
                q = q_ref[classes[0] * per + i * sub:classes[0] * per + i * sub + ncls * sub, hs]
                k = window((kp_ref, kc_ref, kn_ref), classes, lo, hi, hs)
                scores.append(lax.dot_general(q, k, (((1,), (1,)), ((), ())), preferred_element_type=F32))
            stats = []
            for s in scores:
                s = jnp.where(mask, s, NEG_BIG)
                m = jnp.max(s, axis=-1, keepdims=True)
                p = jnp.exp(s - m)
                stats.append((m, jnp.sum(p, axis=-1, keepdims=True), p.astype(BF16)))
            for (classes, h), (m, l, p) in zip(batch, stats):
                hs = slice(h * B_HEAD_DIM, (h + 1) * B_HEAD_DIM)
                v = window((vp_ref, vc_ref, vn_ref), classes, lo, hi, hs)
                o = jnp.dot(p, v, preferred_element_type=F32) / l
                lse = jnp.broadcast_to(m + jnp.log(l), (ncls * sub, B_HEAD_DIM))
                for n, rho in enumerate(classes):
                    out_rows = pl.ds(i * sub * dilation + rho, sub, stride=dilation) if dilation > 1 \
                        else pl.ds(i * sub, sub)
                    of_ref[h, out_rows, :] = o[n * sub:(n + 1) * sub]
                    lf_ref[h, out_rows, :] = lse[n * sub:(n + 1) * sub]
    for h in range(B_HEADS_PER_GROUP):
        hs = slice(h * B_HEAD_DIM, (h + 1) * B_HEAD_DIM)
        o_ref[:, hs] = of_ref[h].astype(o_ref.dtype)
        lse_ref[:, hs] = lf_ref[h]


def _band_attn(z, batch, seq, group):
    dilation = B_GROUPS[group][1]
    assert B_GROUPS[group][0] // (2 * dilation) == BAND_HALF
    nt = seq // TILE

    def tile(blk, shift):
        return pl.BlockSpec((TILE, COL_BLK),
                            lambda b, t: (b * nt + jnp.clip(t + shift, 0, nt - 1), blk + group))

    out_spec = pl.BlockSpec((TILE, B_OUT_W), lambda b, t: (b * nt + t, 0))
    kern = functools.partial(_band_attn_kernel, dilation=dilation, length=seq // dilation,
                             chains=BAND_CHAINS[group])
    return pl.pallas_call(
        kern,
        grid=(batch, nt),
        in_specs=[tile(QB_BLK, 0), tile(KB_BLK, -1), tile(KB_BLK, 0), tile(KB_BLK, 1),
                  tile(VB_BLK, -1), tile(VB_BLK, 0), tile(VB_BLK, 1)],
        out_specs=[out_spec, out_spec],
        out_shape=[jax.ShapeDtypeStruct((batch * seq, B_OUT_W), BF16),
                   jax.ShapeDtypeStruct((batch * seq, B_OUT_W), F32)],
        scratch_shapes=[pltpu.VMEM((B_HEADS_PER_GROUP, TILE, B_HEAD_DIM), F32),
                        pltpu.VMEM((B_HEADS_PER_GROUP, TILE, B_HEAD_DIM), F32)],
        compiler_params=_params(2),
        name=f"band_attn_g{group}",
    )(z, z, z, z, z, z, z)


def _merge_out_kernel(x_ref, oa_ref, o0_ref, o1_ref, o2_ref, l0_ref, l1_ref, l2_ref, *rest):
    n_c = D_MODEL // COL_BLK
    ga_refs, gb_refs = rest[:n_c], rest[n_c:2 * n_c]
    wpa_ref, wpb_ref, wout_ref, y_ref = rest[2 * n_c:]
    l0, l1, l2 = l0_ref[...], l1_ref[...], l2_ref[...]
    mx = jnp.maximum(jnp.maximum(l0, l1), l2)
    e0, e1, e2 = jnp.exp(l0 - mx), jnp.exp(l1 - mx), jnp.exp(l2 - mx)
    num = (e0 * o0_ref[...].astype(F32) + e1 * o1_ref[...].astype(F32) + e2 * o2_ref[...].astype(F32))
    ob = (num / (e0 + e1 + e2)).astype(BF16)
    ya = jnp.dot(oa_ref[...], wpa_ref[...], preferred_element_type=F32)
    yb = jnp.dot(ob, wpb_ref[...], preferred_element_type=F32)
    merged = []
    for c in range(n_c):
        cs = slice(c * COL_BLK, (c + 1) * COL_BLK)
        merged.append((jax.nn.sigmoid(ga_refs[c][...].astype(F32)) * ya[:, cs]
                       + jax.nn.sigmoid(gb_refs[c][...].astype(F32)) * yb[:, cs]).astype(BF16))
    merged = jnp.concatenate(merged, axis=1)
    y_ref[...] = x_ref[...] + jnp.dot(merged, wout_ref[...], preferred_element_type=F32)


def _merge_out(x2d, oa, ob_groups, z, wpa, wpb, wout, tm):
    tokens = x2d.shape[0]
    row = lambda w: pl.BlockSpec((tm, w), lambda i: (i, 0))
    gate = lambda blk: pl.BlockSpec((tm, COL_BLK), lambda i: (i, blk))
    whole = lambda a: pl.BlockSpec(a.shape, lambda i: (0, 0), pipeline_mode=pl.Buffered(1))
    n_c = D_MODEL // COL_BLK
    (o0, l0), (o1, l1), (o2, l2) = ob_groups
    return pl.pallas_call(
        _merge_out_kernel,
        grid=(tokens // tm,),
        in_specs=[row(D_MODEL), row(A_HEADS * A_V_DIM)] + [row(B_OUT_W)] * 6
        + [gate(GA_BLK + c) for c in range(n_c)] + [gate(GB_BLK + c) for c in range(n_c)]
        + [whole(wpa), whole(wpb), whole(wout)],
        out_specs=row(D_MODEL),
        out_shape=jax.ShapeDtypeStruct((tokens, D_MODEL), F32),
        compiler_params=_params(1),
        name="merge_out",
    )(x2d, oa, o0, o1, o2, l0, l1, l2, *([z] * (2 * n_c)), wpa, wpb, wout)


def _ffn_kernel(x_ref, g_ref, w1_ref, w2_ref, gf_ref, y_ref, h_ref, acc_ref, *, final_norm):
    f = pl.program_id(1)

    @pl.when(f == 0)
    def _():
        x = x_ref[...]
        ms = jnp.mean(x * x, axis=-1, keepdims=True)
        h_ref[...] = (x * lax.rsqrt(ms + NORM_EPS) * g_ref[...]).astype(h_ref.dtype)
        acc_ref[...] = x

    u = jnp.dot(h_ref[...], w1_ref[...], preferred_element_type=F32)
    u = jnp.square(jnp.maximum(u, 0.0)).astype(BF16)
    acc_ref[...] += jnp.dot(u, w2_ref[...], preferred_element_type=F32)

    @pl.when(f == pl.num_programs(1) - 1)
    def _():
        y = acc_ref[...]
        if final_norm:
            ms = jnp.mean(y * y, axis=-1, keepdims=True)
            y = y * lax.rsqrt(ms + NORM_EPS) * gf_ref[...]
        y_ref[...] = y


def _ffn(x2d, g, w1, w2, g_final, final_norm, tm, tf):
    tokens = x2d.shape[0]
    kern = functools.partial(_ffn_kernel, final_norm=final_norm)
    return pl.pallas_call(
        kern,
        grid=(tokens // tm, D_FF // tf),
        in_specs=[
            pl.BlockSpec((tm, D_MODEL), lambda i, f: (i, 0)),
            pl.BlockSpec((1, D_MODEL), lambda i, f: (0, 0)),
            pl.BlockSpec((D_MODEL, tf), lambda i, f: (0, f)),
            pl.BlockSpec((tf, D_MODEL), lambda i, f: (f, 0)),
            pl.BlockSpec((1, D_MODEL), lambda i, f: (0, 0)),
        ],
        out_specs=pl.BlockSpec((tm, D_MODEL), lambda i, f: (i, 0)),
        out_shape=jax.ShapeDtypeStruct((tokens, D_MODEL), F32),
        scratch_shapes=[pltpu.VMEM((tm, D_MODEL), BF16), pltpu.VMEM((tm, D_MODEL), F32)],
        compiler_params=_params(2),
        name="ffn",
    )(x2d, g, w1, w2, g_final)


def _trunk(x, norm_mix, w_in, lambda_q1, lambda_k1, lambda_q2, lambda_k2, subln_g,
           w_proj_a, w_proj_b, w_out, norm_ffn, w1, w2, norm_final):
    batch, seq, _ = x.shape
    assert seq % TILE == 0
    depth = w_in.shape[0]
    x2d = x.reshape(batch * seq, D_MODEL)
    tabs_a = _rope_tables(seq, A_QK_DIM)
    tabs_b = tuple(jnp.stack([_sort_rows(t, dilation) for _, dilation in B_GROUPS])
                   for t in _rope_tables(seq, B_HEAD_DIM))
    row = lambda v: v.reshape(1, -1)
    for l in range(depth):
        lambda_init = 0.8 - 0.6 * math.exp(-0.3 * l)
        z = _in_proj(x2d, seq, row(norm_mix[l]), w_in[l], tabs_a, tabs_b)
        oa = _diff_attn(z, batch, seq, row(lambda_q1[l]), row(lambda_k1[l]), row(lambda_q2[l]),
                        row(lambda_k2[l]), row(subln_g[l]), lambda_init, tq=512, tk=1024)
        ob_groups = [_band_attn(z, batch, seq, g) for g in range(N_GROUPS)]
        x2d = _merge_out(x2d, oa, ob_groups, z, w_proj_a[l], w_proj_b[l], w_out[l], tm=256)
        x2d = _ffn(x2d, row(norm_ffn[l]), w1[l], w2[l], row(norm_final),
                   final_norm=(l == depth - 1), tm=512, tf=1024)
    return x2d.reshape(batch, seq, D_MODEL)


def kernel(x_prompt, x_sample, norm_mix, w_in, lambda_q1, lambda_k1, lambda_q2, lambda_k2, subln_g,
           w_proj_a, w_proj_b, w_out, norm_ffn, w1, w2, norm_final):
    weights = (norm_mix, _permute_rotary_columns(w_in).astype(BF16), lambda_q1, lambda_k1, lambda_q2, lambda_k2, subln_g,
               w_proj_a.astype(BF16), w_proj_b.astype(BF16), w_out.astype(BF16), norm_ffn,
               w1.astype(BF16), w2.astype(BF16), norm_final)
    return _trunk(x_prompt, *weights), _trunk(x_sample, *weights)
```

```python
import functools
import math

import jax
import jax.numpy as jnp
import numpy as np
from jax import lax
from jax.experimental import pallas as pl
from jax.experimental.pallas import tpu as pltpu

D_MODEL = 2048
A_HEADS = 8
A_QK_DIM = 64
A_V_DIM = 2 * A_QK_DIM
B_GROUPS = ((128, 1), (512, 4), (2048, 16))
B_HEADS_PER_GROUP = 4
B_HEAD_DIM = 128
B_OUT_W = B_HEADS_PER_GROUP * B_HEAD_DIM
D_FF = 4 * D_MODEL
ROPE_THETA = 500000.0
ROPE_FRACTION_DEN = 4
NORM_EPS = 1e-6
SUBLN_EPS = 1e-5
NEG_BIG = -1e30
LOG2E = 1.4426950408889634

COL_BLK = 512
IN_WIDTH = 11776
N_COL_BLKS = IN_WIDTH // COL_BLK
QA_BLK, KA_BLK, VA_BLK = 0, 2, 4
QB_BLK, KB_BLK, VB_BLK = 6, 9, 12
GA_BLK, GB_BLK = 15, 19
N_GROUPS = len(B_GROUPS)
LANES = 128
TILE = 1024
BAND_HALF = 64
BAND_SUB = 128
BAND_CHAINS = (1, 1, 16)
VT_ROWS = A_V_DIM + 16

VMEM_LIMIT = 56 * 1024 * 1024

BF16 = jnp.bfloat16
F32 = jnp.float32


def _params(n_axes):
    return pltpu.CompilerParams(dimension_semantics=("arbitrary",) * n_axes,
                                vmem_limit_bytes=VMEM_LIMIT)


def _sort_rows(a, dilation):
    n, w = a.shape
    return a.reshape(n // TILE, TILE // dilation, dilation, w).transpose(0, 2, 1, 3).reshape(n, w)


def _rotary_layout(head_dim):
    half = head_dim // ROPE_FRACTION_DEN // 2
    heads = LANES // head_dim
    first = [h * head_dim + i for h in range(heads) for i in range(half)]
    second = [h * head_dim + half + i for h in range(heads) for i in range(half)]
    target1 = list(range(len(first)))
    target2 = list(range(LANES // 2, LANES // 2 + len(second)))
    src = np.arange(LANES)
    freq = np.full(LANES, -1)
    sign = np.zeros(LANES)
    for lanes, cols, sgn in ((target1, first, -1.0), (target2, second, 1.0)):
        for j, (lane, col) in enumerate(zip(lanes, cols)):
            src[lane], freq[lane], sign[lane] = col, j % half, sgn
    rotary, targets = set(first + second), set(target1 + target2)
    for lane, col in zip(sorted(rotary - targets), sorted(targets - rotary)):
        src[lane] = col
    assert sorted(src) == list(range(LANES))
    return src, freq, sign


def _runs(idx):
    out, start = [], 0
    for i in range(1, len(idx) + 1):
        if i == len(idx) or idx[i] != idx[i - 1] + 1:
            out.append((int(idx[start]), int(idx[i - 1]) + 1))
            start = i
    return out


def _permute_rotary_columns(w):
    def section(lo, hi, head_dim):
        ws = w[..., lo:hi].reshape(w.shape[:-1] + ((hi - lo) // LANES, LANES))
        src = _rotary_layout(head_dim)[0]
        ws = jnp.concatenate([ws[..., a:b] for a, b in _runs(src)], axis=-1)
        return ws.reshape(w.shape[:-1] + (hi - lo,))
    a_end, b_lo, b_end = VA_BLK * COL_BLK, QB_BLK * COL_BLK, VB_BLK * COL_BLK
    return jnp.concatenate([section(0, a_end, A_QK_DIM), w[..., a_end:b_lo],
                            section(b_lo, b_end, B_HEAD_DIM), w[..., b_end:]], axis=-1)


def _rope_tables(seq, head_dim):
    rot = head_dim // ROPE_FRACTION_DEN
    pos = jnp.arange(seq, dtype=F32)
    inv = ROPE_THETA ** (-jnp.arange(0, rot, 2, dtype=F32) / rot)
    ang = pos[:, None] * inv[None, :]
    cos, sin = jnp.cos(ang), jnp.sin(ang)
    _, freq, sign = _rotary_layout(head_dim)
    cos_parts, sin_parts, lane = [], [], 0
    while lane < LANES:
        end = lane + 1
        if freq[lane] < 0:
            while end < LANES and freq[end] < 0:
                end += 1
            cos_parts.append(jnp.ones((seq, end - lane), F32))
            sin_parts.append(jnp.zeros((seq, end - lane), F32))
        else:
            while end < LANES and freq[end] == freq[end - 1] + 1:
                end += 1
            cols = slice(int(freq[lane]), int(freq[end - 1]) + 1)
            cos_parts.append(cos[:, cols])
            sin_parts.append(sin[:, cols] * float(sign[lane]))
        lane = end
    return jnp.concatenate(cos_parts, axis=1), jnp.concatenate(sin_parts, axis=1)


def _rope_store(acc, cos, sin, scale, z_ref):
    for c in range(COL_BLK // LANES):
        xc = acc[:, c * LANES:(c + 1) * LANES]
        out = xc * cos + pltpu.roll(xc, LANES // 2, 1) * sin
        if scale != 1.0:
            out = out * scale
        z_ref[:, c * LANES:(c + 1) * LANES] = out.astype(z_ref.dtype)


def _row_order(j):
    return jnp.where((j >= QB_BLK) & (j < GA_BLK), (j - QB_BLK) % N_GROUPS, 0)


def _in_proj_kernel(x_ref, g_ref, w_ref, ca_ref, sa_ref, cb_ref, sb_ref, z_ref, h_ref, slab_ref):
    j = pl.program_id(1)

    @pl.when(j == 0)
    def _():
        x = x_ref[...]
        rinv = lax.rsqrt(jnp.mean(x * x, axis=-1, keepdims=True) + NORM_EPS)
        for c in range(D_MODEL // LANES):
            cs = slice(c * LANES, (c + 1) * LANES)
            slab = x_ref[:, cs] * rinv * g_ref[:, cs]
            h_ref[0, :, cs] = slab.astype(h_ref.dtype)
            slab_ref[...] = slab
            for order in range(1, N_GROUPS):
                dilation = B_GROUPS[order][1]
                per = TILE // dilation
                for rho in range(dilation):
                    h_ref[order, rho * per:(rho + 1) * per, cs] = (
                        slab_ref[pl.ds(rho, per, stride=dilation), :].astype(h_ref.dtype))

    def project():
        return jnp.dot(h_ref[_row_order(j)], w_ref[...], preferred_element_type=F32)

    @pl.when(j < KA_BLK)
    def _():
        _rope_store(project(), ca_ref[...], sa_ref[...], A_QK_DIM ** -0.5 * LOG2E, z_ref)

    @pl.when((j >= KA_BLK) & (j < VA_BLK))
    def _():
        _rope_store(project(), ca_ref[...], sa_ref[...], 1.0, z_ref)

    @pl.when((j >= QB_BLK) & (j < KB_BLK))
    def _():
        _rope_store(project(), cb_ref[...], sb_ref[...], B_HEAD_DIM ** -0.5, z_ref)

    @pl.when((j >= KB_BLK) & (j < VB_BLK))
    def _():
        _rope_store(project(), cb_ref[...], sb_ref[...], 1.0, z_ref)

    @pl.when(((j >= VA_BLK) & (j < QB_BLK)) | (j >= VB_BLK))
    def _():
        z_ref[...] = project().astype(z_ref.dtype)


def _in_proj(x2d, seq, g, w_bf16, tabs_a, tabs_b):
    tokens = x2d.shape[0]
    nt = seq // TILE
    tab_a = pl.BlockSpec((TILE, LANES), lambda i, j: (i % nt, 0))
    tab_b = pl.BlockSpec((None, TILE, LANES), lambda i, j: (_row_order(j), i % nt, 0))
    return pl.pallas_call(
        _in_proj_kernel,
        grid=(tokens // TILE, N_COL_BLKS),
        in_specs=[
            pl.BlockSpec((TILE, D_MODEL), lambda i, j: (i, 0)),
            pl.BlockSpec((1, D_MODEL), lambda i, j: (0, 0)),
            pl.BlockSpec((D_MODEL, COL_BLK), lambda i, j: (0, j)),
            tab_a, tab_a, tab_b, tab_b,
        ],
        out_specs=pl.BlockSpec((TILE, COL_BLK), lambda i, j: (i, j)),
        out_shape=jax.ShapeDtypeStruct((tokens, IN_WIDTH), BF16),
        scratch_shapes=[pltpu.VMEM((N_GROUPS, TILE, D_MODEL), BF16), pltpu.VMEM((TILE, LANES), F32)],
        compiler_params=_params(2),
        name="in_proj",
    )(x2d, g, w_bf16, *tabs_a, *tabs_b)


_MAP0_LANES = _runs(np.flatnonzero(_rotary_layout(A_QK_DIM)[0] < A_QK_DIM))


def _diff_attn_kernel(q_ref, k_ref, v_ref, lq1_ref, lk1_ref, lq2_ref, lk2_ref, g_ref, o_ref,
                      q2t_ref, vt_ref, m_ref, acc_ref, s_ref, *, tq, tk, seq, lambda_init):
    qi = pl.program_id(2)
    nck = seq // tk

    @pl.when(qi == 0)
    def _():
        def transpose_chunk(c, carry):
            rows = pl.ds(pl.multiple_of(c * tk, tk), tk)
            vt_ref[c, :A_V_DIM, :] = v_ref[rows, :].astype(F32).T.astype(BF16)
            sub = lax.broadcasted_iota(jnp.int32, (VT_ROWS - A_V_DIM, tk), 0)
            vt_ref[c, A_V_DIM:, :] = jnp.where(sub == 0, 1.0, 0.0).astype(BF16)
            return carry
        lax.fori_loop(0, nck, transpose_chunk, 0)

    qt = q_ref[...].astype(F32).T
    dim = lax.broadcasted_iota(jnp.int32, (A_V_DIM, 1), 0)
    map0 = functools.reduce(jnp.logical_or, [(dim >= a) & (dim < b) for a, b in _MAP0_LANES])
    q2t_ref[:, :tq] = jnp.where(map0, qt, 0.0).astype(BF16)
    q2t_ref[:, tq:] = jnp.where(map0, 0.0, qt).astype(BF16)
    m_ref[...] = jnp.full(m_ref.shape, NEG_BIG, F32)
    acc_ref[...] = jnp.zeros(acc_ref.shape, F32)

    def scores(c):
        rows = pl.ds(pl.multiple_of(c * tk, tk), tk)
        return jnp.dot(k_ref[rows, :], q2t_ref[...], preferred_element_type=F32)

    def accumulate(c, s_cur_ref):
        s = s_cur_ref[...]
        m_old = m_ref[0:1, :]
        m_new = jnp.maximum(m_old, jnp.max(s, axis=0, keepdims=True))
        p = jnp.exp2(s - m_new).astype(BF16)
        pv = jnp.dot(vt_ref[c], p, preferred_element_type=F32)
        acc_ref[...] = jnp.exp2(m_old - m_new) * acc_ref[...] + pv
        m_ref[...] = jnp.broadcast_to(m_new, m_ref.shape)

    s_a, s_b = s_ref.at[0], s_ref.at[1]
    s_a[...] = scores(0)

    bufs = (s_a, s_b)
    group = 4 if nck % 4 == 0 else 2

    def trip(c0, last):
        for u in range(group):
            if not (last and u == group - 1):
                bufs[(u + 1) % 2][...] = scores(c0 + u + 1)
            accumulate(c0 + u, bufs[u % 2])

    def body(i, carry):
        trip(group * i, False)
        return carry

    lax.fori_loop(0, nck // group - 1, body, 0)
    trip(nck - group, True)

    acc = acc_ref[...]
    ot = acc[:A_V_DIM] / acc[A_V_DIM:A_V_DIM + 1]
    lam = (jnp.exp(jnp.sum(lq1_ref[...] * lk1_ref[...], axis=-1, keepdims=True))
           - jnp.exp(jnp.sum(lq2_ref[...] * lk2_ref[...], axis=-1, keepdims=True)) + lambda_init)
    o = ot[:, :tq] - lam * ot[:, tq:]
    ms = jnp.mean(o * o, axis=0, keepdims=True)
    o = o * lax.rsqrt(ms + SUBLN_EPS)
    o_ref[...] = (o.T * g_ref[...] * (1.0 - lambda_init)).astype(o_ref.dtype)


def _diff_attn(z, batch, seq, lq1, lk1, lq2, lk2, subln_g, lambda_init, tq, tk):
    tokens = batch * seq
    nq = seq // tq
    assert seq % (2 * tk) == 0
    vec = lambda n: pl.BlockSpec((1, n), lambda b, h, qi: (0, 0))
    kern = functools.partial(_diff_attn_kernel, tq=tq, tk=tk, seq=seq, lambda_init=lambda_init)
    return pl.pallas_call(
        kern,
        grid=(batch, A_HEADS, nq),
        in_specs=[
            pl.BlockSpec((tq, A_V_DIM), lambda b, h, qi: (b * nq + qi, h)),
            pl.BlockSpec((seq, A_V_DIM), lambda b, h, qi: (b, A_HEADS + h)),
            pl.BlockSpec((seq, A_V_DIM), lambda b, h, qi: (b, 2 * A_HEADS + h)),
            vec(A_QK_DIM), vec(A_QK_DIM), vec(A_QK_DIM), vec(A_QK_DIM), vec(A_V_DIM),
        ],
        out_specs=pl.BlockSpec((tq, A_V_DIM), lambda b, h, qi: (b * nq + qi, h)),
        out_shape=jax.ShapeDtypeStruct((tokens, A_HEADS * A_V_DIM), BF16),
        scratch_shapes=[
            pltpu.VMEM((A_V_DIM, 2 * tq), BF16),
            pltpu.VMEM((seq // tk, VT_ROWS, tk), BF16),
            pltpu.VMEM((8, 2 * tq), F32),
            pltpu.VMEM((VT_ROWS, 2 * tq), F32),
            pltpu.VMEM((2, tk, 2 * tq), F32),
        ],
        compiler_params=_params(3),
        name="diff_attn",
    )(z, z, z, lq1, lk1, lq2, lk2, subln_g)


def _band_window(prev_ref, cur_ref, next_ref, base, lo, hi, per, hs):
    parts = []
    if lo < 0:
        parts.append(prev_ref[base + per + lo:base + per + min(hi, 0), hs])
    if hi > 0 and lo < per:
        parts.append(cur_ref[base + max(lo, 0):base + min(hi, per), hs])
    if hi > per:
        parts.append(next_ref[base + max(lo, per) - per:base + hi - per, hs])
    return parts[0] if len(parts) == 1 else jnp.concatenate(parts, axis=0)


def _band_attn_kernel(q_ref, kp_ref, kc_ref, kn_ref, vp_ref, vc_ref, vn_ref, o_ref, lse_ref, of_ref, lf_ref,
                      *, dilation, length, chains):
    t = pl.program_id(1)
    per = TILE // dilation
    sub = min(per, BAND_SUB)
    ncls = BAND_SUB // sub
    nkeys = sub + 2 * BAND_HALF
    shape = (ncls * sub, ncls * nkeys)
    rows = lax.broadcasted_iota(jnp.int32, shape, 0)
    cols = lax.broadcasted_iota(jnp.int32, shape, 1)
    row_cls, row_pos = rows // sub, rows % sub
    col_cls, col_pos = cols // nkeys, cols % nkeys
    in_band = (jnp.abs(col_pos - BAND_HALF - row_pos) <= BAND_HALF) & (row_cls == col_cls)

    def window(refs, classes, lo, hi, hs):
        parts = [_band_window(*refs, rho * per, lo, hi, per, hs) for rho in classes]
        return parts[0] if len(parts) == 1 else jnp.concatenate(parts, axis=0)

    for i in range(per // sub):
        kpos = t * per + i * sub - BAND_HALF + col_pos
        mask = in_band & (kpos >= 0) & (kpos < length)
        lo, hi = i * sub - BAND_HALF, i * sub + sub + BAND_HALF
        tiles = [(range(rho0, rho0 + ncls), h) for rho0 in range(0, dilation, ncls)
                 for h in range(B_HEADS_PER_GROUP)]
        for b0 in range(0, len(tiles), chains):
            batch = tiles[b0:b0 + chains]
            scores = []
            for classes, h in batch:
                hs = slice(h * B_HEAD_DIM, (h + 1) * B_HEAD_DIM)
                q = q_ref[classes[0] * per + i * sub:classes[0] * per + i * sub + ncls * sub, hs]
                k = window((kp_ref, kc_ref, kn_ref), classes, lo, hi, hs)
                scores.append(lax.dot_general(q, k, (((1,), (1,)), ((), ())), preferred_element_type=F32))
            stats = []
            for s in scores:
                s = jnp.where(mask, s, NEG_BIG)
                m = jnp.max(s, axis=-1, keepdims=True)
                p = jnp.exp(s - m)
                stats.append((m, jnp.sum(p, axis=-1, keepdims=True), p.astype(BF16)))
            for (classes, h), (m, l, p) in zip(batch, stats):
                hs = slice(h * B_HEAD_DIM, (h + 1) * B_HEAD_DIM)
                v = window((vp_ref, vc_ref, vn_ref), classes, lo, hi, hs)
                o = jnp.dot(p, v, preferred_element_type=F32) / l
                lse = jnp.broadcast_to(m + jnp.log(l), (ncls * sub, B_HEAD_DIM))
                for n, rho in enumerate(classes):
                    out_rows = pl.ds(i * sub * dilation + rho, sub, stride=dilation) if dilation > 1 \
                        else pl.ds(i * sub, sub)
                    of_ref[h, out_rows, :] = o[n * sub:(n + 1) * sub]
                    lf_ref[h, out_rows, :] = lse[n * sub:(n + 1) * sub]
    for h in range(B_HEADS_PER_GROUP):
        hs = slice(h * B_HEAD_DIM, (h + 1) * B_HEAD_DIM)
        o_ref[:, hs] = of_ref[h].astype(o_ref.dtype)
        lse_ref[:, hs] = lf_ref[h]


def _band_attn(z, batch, seq, group):
    dilation = B_GROUPS[group][1]
    assert B_GROUPS[group][0] // (2 * dilation) == BAND_HALF
    nt = seq // TILE

    def tile(blk, shift):
        return pl.BlockSpec((TILE, COL_BLK),
                            lambda b, t: (b * nt + jnp.clip(t + shift, 0, nt - 1), blk + group))

    out_spec = pl.BlockSpec((TILE, B_OUT_W), lambda b, t: (b * nt + t, 0))
    kern = functools.partial(_band_attn_kernel, dilation=dilation, length=seq // dilation,
                             chains=BAND_CHAINS[group])
    return pl.pallas_call(
        kern,
        grid=(batch, nt),
        in_specs=[tile(QB_BLK, 0), tile(KB_BLK, -1), tile(KB_BLK, 0), tile(KB_BLK, 1),
                  tile(VB_BLK, -1), tile(VB_BLK, 0), tile(VB_BLK, 1)],
        out_specs=[out_spec, out_spec],
        out_shape=[jax.ShapeDtypeStruct((batch * seq, B_OUT_W), BF16),
                   jax.ShapeDtypeStruct((batch * seq, B_OUT_W), F32)],
        scratch_shapes=[pltpu.VMEM((B_HEADS_PER_GROUP, TILE, B_HEAD_DIM), F32),
                        pltpu.VMEM((B_HEADS_PER_GROUP, TILE, B_HEAD_DIM), F32)],
        compiler_params=_params(2),
        name=f"band_attn_g{group}",
    )(z, z, z, z, z, z, z)


def _merge_out_kernel(x_ref, oa_ref, o0_ref, o1_ref, o2_ref, l0_ref, l1_ref, l2_ref, *rest):
    n_c = D_MODEL // COL_BLK
    ga_refs, gb_refs = rest[:n_c], rest[n_c:2 * n_c]
    wpa_ref, wpb_ref, wout_ref, y_ref = rest[2 * n_c:]
    l0, l1, l2 = l0_ref[...], l1_ref[...], l2_ref[...]
    mx = jnp.maximum(jnp.maximum(l0, l1), l2)
    e0, e1, e2 = jnp.exp(l0 - mx), jnp.exp(l1 - mx), jnp.exp(l2 - mx)
    num = (e0 * o0_ref[...].astype(F32) + e1 * o1_ref[...].astype(F32) + e2 * o2_ref[...].astype(F32))
    ob = (num / (e0 + e1 + e2)).astype(BF16)
    ya = jnp.dot(oa_ref[...], wpa_ref[...], preferred_element_type=F32)
    yb = jnp.dot(ob, wpb_ref[...], preferred_element_type=F32)
    merged = []
    for c in range(n_c):
        cs = slice(c * COL_BLK, (c + 1) * COL_BLK)
        merged.append((jax.nn.sigmoid(ga_refs[c][...].astype(F32)) * ya[:, cs]
                       + jax.nn.sigmoid(gb_refs[c][...].astype(F32)) * yb[:, cs]).astype(BF16))
    merged = jnp.concatenate(merged, axis=1)
    y_ref[...] = x_ref[...] + jnp.dot(merged, wout_ref[...], preferred_element_type=F32)


def _merge_out(x2d, oa, ob_groups, z, wpa, wpb, wout, tm):
    tokens = x2d.shape[0]
    row = lambda w: pl.BlockSpec((tm, w), lambda i: (i, 0))
    gate = lambda blk: pl.BlockSpec((tm, COL_BLK), lambda i: (i, blk))
    whole = lambda a: pl.BlockSpec(a.shape, lambda i: (0, 0), pipeline_mode=pl.Buffered(1))
    n_c = D_MODEL // COL_BLK
    (o0, l0), (o1, l1), (o2, l2) = ob_groups
    return pl.pallas_call(
        _merge_out_kernel,
        grid=(tokens // tm,),
        in_specs=[row(D_MODEL), row(A_HEADS * A_V_DIM)] + [row(B_OUT_W)] * 6
        + [gate(GA_BLK + c) for c in range(n_c)] + [gate(GB_BLK + c) for c in range(n_c)]
        + [whole(wpa), whole(wpb), whole(wout)],
        out_specs=row(D_MODEL),
        out_shape=jax.ShapeDtypeStruct((tokens, D_MODEL), F32),
        compiler_params=_params(1),
        name="merge_out",
    )(x2d, oa, o0, o1, o2, l0, l1, l2, *([z] * (2 * n_c)), wpa, wpb, wout)


def _ffn_kernel(x_ref, g_ref, w1_ref, w2_ref, gf_ref, y_ref, h_ref, acc_ref, *, final_norm):
    f = pl.program_id(1)

    @pl.when(f == 0)
    def _():
        x = x_ref[...]
        ms = jnp.mean(x * x, axis=-1, keepdims=True)
        h_ref[...] = (x * lax.rsqrt(ms + NORM_EPS) * g_ref[...]).astype(h_ref.dtype)
        acc_ref[...] = x

    u = jnp.dot(h_ref[...], w1_ref[...], preferred_element_type=F32)
    u = jnp.square(jnp.maximum(u, 0.0)).astype(BF16)
    acc_ref[...] += jnp.dot(u, w2_ref[...], preferred_element_type=F32)

    @pl.when(f == pl.num_programs(1) - 1)
    def _():
        y = acc_ref[...]
        if final_norm:
            ms = jnp.mean(y * y, axis=-1, keepdims=True)
            y = y * lax.rsqrt(ms + NORM_EPS) * gf_ref[...]
        y_ref[...] = y


def _ffn(x2d, g, w1, w2, g_final, final_norm, tm, tf):
    tokens = x2d.shape[0]
    kern = functools.partial(_ffn_kernel, final_norm=final_norm)
    return pl.pallas_call(
        kern,
        grid=(tokens // tm, D_FF // tf),
        in_specs=[
            pl.BlockSpec((tm, D_MODEL), lambda i, f: (i, 0)),
            pl.BlockSpec((1, D_MODEL), lambda i, f: (0, 0)),
            pl.BlockSpec((D_MODEL, tf), lambda i, f: (0, f)),
            pl.BlockSpec((tf, D_MODEL), lambda i, f: (f, 0)),
            pl.BlockSpec((1, D_MODEL), lambda i, f: (0, 0)),
        ],
        out_specs=pl.BlockSpec((tm, D_MODEL), lambda i, f: (i, 0)),
        out_shape=jax.ShapeDtypeStruct((tokens, D_MODEL), F32),
        scratch_shapes=[pltpu.VMEM((tm, D_MODEL), BF16), pltpu.VMEM((tm, D_MODEL), F32)],
        compiler_params=_params(2),
        name="ffn",
    )(x2d, g, w1, w2, g_final)


def _position_tables(seq):
    tabs_a = _rope_tables(seq, A_QK_DIM)
    tabs_b = tuple(jnp.stack([_sort_rows(t, dilation) for _, dilation in B_GROUPS])
                   for t in _rope_tables(seq, B_HEAD_DIM))
    return tabs_a, tabs_b


def _trunk(x, tabs, norm_mix, w_in, lambda_q1, lambda_k1, lambda_q2, lambda_k2, subln_g,
           w_proj_a, w_proj_b, w_out, norm_ffn, w1, w2, norm_final):
    batch, seq, _ = x.shape
    assert seq % TILE == 0
    depth = w_in.shape[0]
    x2d = x.reshape(batch * seq, D_MODEL)
    tabs_a, tabs_b = tabs
    row = lambda v: v.reshape(1, -1)
    for l in range(depth):
        lambda_init = 0.8 - 0.6 * math.exp(-0.3 * l)
        z = _in_proj(x2d, seq, row(norm_mix[l]), w_in[l], tabs_a, tabs_b)
        oa = _diff_attn(z, batch, seq, row(lambda_q1[l]), row(lambda_k1[l]), row(lambda_q2[l]),
                        row(lambda_k2[l]), row(subln_g[l]), lambda_init, tq=512, tk=1024)
        ob_groups = [_band_attn(z, batch, seq, g) for g in range(N_GROUPS)]
        x2d = _merge_out(x2d, oa, ob_groups, z, w_proj_a[l], w_proj_b[l], w_out[l], tm=256)
        x2d = _ffn(x2d, row(norm_ffn[l]), w1[l], w2[l], row(norm_final),
                   final_norm=(l == depth - 1), tm=512, tf=1024)
    return x2d.reshape(batch, seq, D_MODEL)


def kernel(x_prompt, x_sample, norm_mix, w_in, lambda_q1, lambda_k1, lambda_q2, lambda_k2, subln_g,
           w_proj_a, w_proj_b, w_out, norm_ffn, w1, w2, norm_final):
    weights = (norm_mix, _permute_rotary_columns(w_in.astype(BF16)), lambda_q1, lambda_k1, lambda_q2,
               lambda_k2, subln_g, w_proj_a.astype(BF16), w_proj_b.astype(BF16), w_out.astype(BF16), norm_ffn,
               w1.astype(BF16), w2.astype(BF16), norm_final)
    tabs = _position_tables(max(x_prompt.shape[1], x_sample.shape[1]))
    return _trunk(x_prompt, tabs, *weights), _trunk(x_sample, tabs, *weights)
```

```python
import functools
import math

import jax
import jax.numpy as jnp
import numpy as np
from jax import lax
from jax.experimental import pallas as pl
from jax.experimental.pallas import tpu as pltpu

D_MODEL = 2048
A_HEADS = 8
A_QK_DIM = 64
A_V_DIM = 2 * A_QK_DIM
B_GROUPS = ((128, 1), (512, 4), (2048, 16))
B_HEADS_PER_GROUP = 4
B_HEAD_DIM = 128
B_OUT_W = B_HEADS_PER_GROUP * B_HEAD_DIM
D_FF = 4 * D_MODEL
ROPE_THETA = 500000.0
ROPE_FRACTION_DEN = 4
NORM_EPS = 1e-6
SUBLN_EPS = 1e-5
NEG_BIG = -1e30
LOG2E = 1.4426950408889634

COL_BLK = 512
IN_WIDTH = 11776
N_COL_BLKS = IN_WIDTH // COL_BLK
QA_BLK, KA_BLK, VA_BLK = 0, 2, 4
QB_BLK, KB_BLK, VB_BLK = 6, 9, 12
GA_BLK, GB_BLK = 15, 19
N_GROUPS = len(B_GROUPS)
LANES = 128
TILE = 1024
BAND_HALF = 64
BAND_SUB = 128
BAND_CHAINS = (1, 1, 16)
VT_ROWS = A_V_DIM + 16

VMEM_LIMIT = 56 * 1024 * 1024

BF16 = jnp.bfloat16
F32 = jnp.float32


def _params(n_axes):
    return pltpu.CompilerParams(dimension_semantics=("arbitrary",) * n_axes,
                                vmem_limit_bytes=VMEM_LIMIT)


def _sort_rows(a, dilation):
    n, w = a.shape
    return a.reshape(n // TILE, TILE // dilation, dilation, w).transpose(0, 2, 1, 3).reshape(n, w)


def _rotary_layout(head_dim):
    half = head_dim // ROPE_FRACTION_DEN // 2
    heads = LANES // head_dim
    first = [h * head_dim + i for h in range(heads) for i in range(half)]
    second = [h * head_dim + half + i for h in range(heads) for i in range(half)]
    target1 = list(range(len(first)))
    target2 = list(range(LANES // 2, LANES // 2 + len(second)))
    src = np.arange(LANES)
    freq = np.full(LANES, -1)
    sign = np.zeros(LANES)
    for lanes, cols, sgn in ((target1, first, -1.0), (target2, second, 1.0)):
        for j, (lane, col) in enumerate(zip(lanes, cols)):
            src[lane], freq[lane], sign[lane] = col, j % half, sgn
    rotary, targets = set(first + second), set(target1 + target2)
    for lane, col in zip(sorted(rotary - targets), sorted(targets - rotary)):
        src[lane] = col
    assert sorted(src) == list(range(LANES))
    return src, freq, sign


def _runs(idx):
    out, start = [], 0
    for i in range(1, len(idx) + 1):
        if i == len(idx) or idx[i] != idx[i - 1] + 1:
            out.append((int(idx[start]), int(idx[i - 1]) + 1))
            start = i
    return out


def _permute_rotary_columns(w):
    def section(lo, hi, head_dim):
        ws = w[..., lo:hi].reshape(w.shape[:-1] + ((hi - lo) // LANES, LANES))
        src = _rotary_layout(head_dim)[0]
        ws = jnp.concatenate([ws[..., a:b] for a, b in _runs(src)], axis=-1)
        return ws.reshape(w.shape[:-1] + (hi - lo,))
    a_end, b_lo, b_end = VA_BLK * COL_BLK, QB_BLK * COL_BLK, VB_BLK * COL_BLK
    return jnp.concatenate([section(0, a_end, A_QK_DIM), w[..., a_end:b_lo],
                            section(b_lo, b_end, B_HEAD_DIM), w[..., b_end:]], axis=-1)


def _rope_tables(seq, head_dim):
    rot = head_dim // ROPE_FRACTION_DEN
    pos = jnp.arange(seq, dtype=F32)
    inv = ROPE_THETA ** (-jnp.arange(0, rot, 2, dtype=F32) / rot)
    _, freq, sign = _rotary_layout(head_dim)
    inv_lane = jnp.where(jnp.asarray(freq >= 0), inv[np.maximum(freq, 0)], 0.0)
    ang = pos[:, None] * inv_lane[None, :]
    return jnp.cos(ang), jnp.sin(ang) * jnp.asarray(sign, F32)[None, :]


def _rope_store(acc, cos, sin, scale, z_ref):
    for c in range(COL_BLK // LANES):
        xc = acc[:, c * LANES:(c + 1) * LANES]
        out = xc * cos + pltpu.roll(xc, LANES // 2, 1) * sin
        if scale != 1.0:
            out = out * scale
        z_ref[:, c * LANES:(c + 1) * LANES] = out.astype(z_ref.dtype)


def _row_order(j):
    return jnp.where((j >= QB_BLK) & (j < GA_BLK), (j - QB_BLK) % N_GROUPS, 0)


def _in_proj_kernel(x_ref, g_ref, w_ref, ca_ref, sa_ref, cb_ref, sb_ref, z_ref, h_ref, slab_ref):
    j = pl.program_id(1)

    @pl.when(j == 0)
    def _():
        x = x_ref[...]
        rinv = lax.rsqrt(jnp.mean(x * x, axis=-1, keepdims=True) + NORM_EPS)
        for c in range(D_MODEL // LANES):
            cs = slice(c * LANES, (c + 1) * LANES)
            slab = x_ref[:, cs] * rinv * g_ref[:, cs]
            h_ref[0, :, cs] = slab.astype(h_ref.dtype)
            slab_ref[...] = slab
            for order in range(1, N_GROUPS):
                dilation = B_GROUPS[order][1]
                per = TILE // dilation
                for rho in range(dilation):
                    h_ref[order, rho * per:(rho + 1) * per, cs] = (
                        slab_ref[pl.ds(rho, per, stride=dilation), :].astype(h_ref.dtype))

    def project():
        return jnp.dot(h_ref[_row_order(j)], w_ref[...], preferred_element_type=F32)

    @pl.when(j < KA_BLK)
    def _():
        _rope_store(project(), ca_ref[...], sa_ref[...], A_QK_DIM ** -0.5 * LOG2E, z_ref)

    @pl.when((j >= KA_BLK) & (j < VA_BLK))
    def _():
        _rope_store(project(), ca_ref[...], sa_ref[...], 1.0, z_ref)

    @pl.when((j >= QB_BLK) & (j < KB_BLK))
    def _():
        _rope_store(project(), cb_ref[...], sb_ref[...], B_HEAD_DIM ** -0.5, z_ref)

    @pl.when((j >= KB_BLK) & (j < VB_BLK))
    def _():
        _rope_store(project(), cb_ref[...], sb_ref[...], 1.0, z_ref)

    @pl.when(((j >= VA_BLK) & (j < QB_BLK)) | (j >= VB_BLK))
    def _():
        z_ref[...] = project().astype(z_ref.dtype)


def _in_proj(x2d, seq, g, w_bf16, tabs_a, tabs_b):
    tokens = x2d.shape[0]
    nt = seq // TILE
    tab_a = pl.BlockSpec((TILE, LANES), lambda i, j: (i % nt, 0))
    tab_b = pl.BlockSpec((None, TILE, LANES), lambda i, j: (_row_order(j), i % nt, 0))
    return pl.pallas_call(
        _in_proj_kernel,
        grid=(tokens // TILE, N_COL_BLKS),
        in_specs=[
            pl.BlockSpec((TILE, D_MODEL), lambda i, j: (i, 0)),
            pl.BlockSpec((1, D_MODEL), lambda i, j: (0, 0)),
            pl.BlockSpec((None, D_MODEL, COL_BLK), lambda i, j: (j, 0, 0)),
            tab_a, tab_a, tab_b, tab_b,
        ],
        out_specs=pl.BlockSpec((TILE, COL_BLK), lambda i, j: (i, j)),
        out_shape=jax.ShapeDtypeStruct((tokens, IN_WIDTH), BF16),
        scratch_shapes=[pltpu.VMEM((N_GROUPS, TILE, D_MODEL), BF16), pltpu.VMEM((TILE, LANES), F32)],
        compiler_params=_params(2),
        name="in_proj",
    )(x2d, g, w_bf16, *tabs_a, *tabs_b)


_MAP0_LANES = _runs(np.flatnonzero(_rotary_layout(A_QK_DIM)[0] < A_QK_DIM))


def _diff_attn_kernel(q_ref, k_ref, v_ref, lq1_ref, lk1_ref, lq2_ref, lk2_ref, g_ref, o_ref,
                      q2t_ref, vt_ref, m_ref, acc_ref, s_ref, *, tq, tk, seq, lambda_init):
    qi = pl.program_id(2)
    nck = seq // tk

    @pl.when(qi == 0)
    def _():
        def transpose_chunk(c, carry):
            rows = pl.ds(pl.multiple_of(c * tk, tk), tk)
            vt_ref[c, :A_V_DIM, :] = v_ref[rows, :].astype(F32).T.astype(BF16)
            sub = lax.broadcasted_iota(jnp.int32, (VT_ROWS - A_V_DIM, tk), 0)
            vt_ref[c, A_V_DIM:, :] = jnp.where(sub == 0, 1.0, 0.0).astype(BF16)
            return carry
        lax.fori_loop(0, nck, transpose_chunk, 0)

    qt = q_ref[...].astype(F32).T
    dim = lax.broadcasted_iota(jnp.int32, (A_V_DIM, 1), 0)
    map0 = functools.reduce(jnp.logical_or, [(dim >= a) & (dim < b) for a, b in _MAP0_LANES])
    q2t_ref[:, :tq] = jnp.where(map0, qt, 0.0).astype(BF16)
    q2t_ref[:, tq:] = jnp.where(map0, 0.0, qt).astype(BF16)
    m_ref[...] = jnp.full(m_ref.shape, NEG_BIG, F32)
    acc_ref[...] = jnp.zeros(acc_ref.shape, F32)

    def scores(c):
        rows = pl.ds(pl.multiple_of(c * tk, tk), tk)
        return jnp.dot(k_ref[rows, :], q2t_ref[...], preferred_element_type=F32)

    def accumulate(c, s_cur_ref):
        s = s_cur_ref[...]
        m_old = m_ref[0:1, :]
        m_new = jnp.maximum(m_old, jnp.max(s, axis=0, keepdims=True))
        p = jnp.exp2(s - m_new).astype(BF16)
        pv = jnp.dot(vt_ref[c], p, preferred_element_type=F32)
        acc_ref[...] = jnp.exp2(m_old - m_new) * acc_ref[...] + pv
        m_ref[...] = jnp.broadcast_to(m_new, m_ref.shape)

    s_a, s_b = s_ref.at[0], s_ref.at[1]
    s_a[...] = scores(0)

    bufs = (s_a, s_b)
    group = 4 if nck % 4 == 0 else 2

    def trip(c0, last):
        for u in range(group):
            if not (last and u == group - 1):
                bufs[(u + 1) % 2][...] = scores(c0 + u + 1)
            accumulate(c0 + u, bufs[u % 2])

    def body(i, carry):
        trip(group * i, False)
        return carry

    lax.fori_loop(0, nck // group - 1, body, 0)
    trip(nck - group, True)

    acc = acc_ref[...]
    ot = acc[:A_V_DIM] / acc[A_V_DIM:A_V_DIM + 1]
    lam = (jnp.exp(jnp.sum(lq1_ref[...] * lk1_ref[...], axis=-1, keepdims=True))
           - jnp.exp(jnp.sum(lq2_ref[...] * lk2_ref[...], axis=-1, keepdims=True)) + lambda_init)
    o = ot[:, :tq] - lam * ot[:, tq:]
    ms = jnp.mean(o * o, axis=0, keepdims=True)
    o = o * lax.rsqrt(ms + SUBLN_EPS)
    o_ref[...] = (o.T * g_ref[...] * (1.0 - lambda_init)).astype(o_ref.dtype)


def _diff_attn(z, batch, seq, lq1, lk1, lq2, lk2, subln_g, lambda_init, tq, tk):
    tokens = batch * seq
    nq = seq // tq
    assert seq % (2 * tk) == 0
    vec = lambda n: pl.BlockSpec((1, n), lambda b, h, qi: (0, 0))
    kern = functools.partial(_diff_attn_kernel, tq=tq, tk=tk, seq=seq, lambda_init=lambda_init)
    return pl.pallas_call(
        kern,
        grid=(batch, A_HEADS, nq),
        in_specs=[
            pl.BlockSpec((tq, A_V_DIM), lambda b, h, qi: (b * nq + qi, h)),
            pl.BlockSpec((seq, A_V_DIM), lambda b, h, qi: (b, A_HEADS + h)),
            pl.BlockSpec((seq, A_V_DIM), lambda b, h, qi: (b, 2 * A_HEADS + h)),
            vec(A_QK_DIM), vec(A_QK_DIM), vec(A_QK_DIM), vec(A_QK_DIM), vec(A_V_DIM),
        ],
        out_specs=pl.BlockSpec((tq, A_V_DIM), lambda b, h, qi: (b * nq + qi, h)),
        out_shape=jax.ShapeDtypeStruct((tokens, A_HEADS * A_V_DIM), BF16),
        scratch_shapes=[
            pltpu.VMEM((A_V_DIM, 2 * tq), BF16),
            pltpu.VMEM((seq // tk, VT_ROWS, tk), BF16),
            pltpu.VMEM((8, 2 * tq), F32),
            pltpu.VMEM((VT_ROWS, 2 * tq), F32),
            pltpu.VMEM((2, tk, 2 * tq), F32),
        ],
        compiler_params=_params(3),
        name="diff_attn",
    )(z, z, z, lq1, lk1, lq2, lk2, subln_g)


def _band_window(prev_ref, cur_ref, next_ref, base, lo, hi, per, hs):
    parts = []
    if lo < 0:
        parts.append(prev_ref[base + per + lo:base + per + min(hi, 0), hs])
    if hi > 0 and lo < per:
        parts.append(cur_ref[base + max(lo, 0):base + min(hi, per), hs])
    if hi > per:
        parts.append(next_ref[base + max(lo, per) - per:base + hi - per, hs])
    return parts[0] if len(parts) == 1 else jnp.concatenate(parts, axis=0)


def _band_attn_kernel(q_ref, kp_ref, kc_ref, kn_ref, vp_ref, vc_ref, vn_ref, o_ref, lse_ref, of_ref, lf_ref,
                      *, dilation, length, chains):
    t = pl.program_id(1)
    per = TILE // dilation
    sub = min(per, BAND_SUB)
    ncls = BAND_SUB // sub
    nkeys = sub + 2 * BAND_HALF
    shape = (ncls * sub, ncls * nkeys)
    rows = lax.broadcasted_iota(jnp.int32, shape, 0)
    cols = lax.broadcasted_iota(jnp.int32, shape, 1)
    row_cls, row_pos = rows // sub, rows % sub
    col_cls, col_pos = cols // nkeys, cols % nkeys
    in_band = (jnp.abs(col_pos - BAND_HALF - row_pos) <= BAND_HALF) & (row_cls == col_cls)

    def window(refs, classes, lo, hi, hs):
        parts = [_band_window(*refs, rho * per, lo, hi, per, hs) for rho in classes]
        return parts[0] if len(parts) == 1 else jnp.concatenate(parts, axis=0)

    for i in range(per // sub):
        kpos = t * per + i * sub - BAND_HALF + col_pos
        mask = in_band & (kpos >= 0) & (kpos < length)
        lo, hi = i * sub - BAND_HALF, i * sub + sub + BAND_HALF
        tiles = [(range(rho0, rho0 + ncls), h) for rho0 in range(0, dilation, ncls)
                 for h in range(B_HEADS_PER_GROUP)]
        for b0 in range(0, len(tiles), chains):
            batch = tiles[b0:b0 + chains]
            scores = []
            for classes, h in batch:
                hs = slice(h * B_HEAD_DIM, (h + 1) * B_HEAD_DIM)
                q = q_ref[classes[0] * per + i * sub:classes[0] * per + i * sub + ncls * sub, hs]
                k = window((kp_ref, kc_ref, kn_ref), classes, lo, hi, hs)
                scores.append(lax.dot_general(q, k, (((1,), (1,)), ((), ())), preferred_element_type=F32))
            stats = []
            for s in scores:
                s = jnp.where(mask, s, NEG_BIG)
                m = jnp.max(s, axis=-1, keepdims=True)
                p = jnp.exp(s - m)
                stats.append((m, jnp.sum(p, axis=-1, keepdims=True), p.astype(BF16)))
            for (classes, h), (m, l, p) in zip(batch, stats):
                hs = slice(h * B_HEAD_DIM, (h + 1) * B_HEAD_DIM)
                v = window((vp_ref, vc_ref, vn_ref), classes, lo, hi, hs)
                o = jnp.dot(p, v, preferred_element_type=F32) / l
                lse = jnp.broadcast_to(m + jnp.log(l), (ncls * sub, B_HEAD_DIM))
                for n, rho in enumerate(classes):
                    out_rows = pl.ds(i * sub * dilation + rho, sub, stride=dilation) if dilation > 1 \
                        else pl.ds(i * sub, sub)
                    of_ref[h, out_rows, :] = o[n * sub:(n + 1) * sub]
                    lf_ref[h, out_rows, :] = lse[n * sub:(n + 1) * sub]
    for h in range(B_HEADS_PER_GROUP):
        hs = slice(h * B_HEAD_DIM, (h + 1) * B_HEAD_DIM)
        o_ref[:, hs] = of_ref[h].astype(o_ref.dtype)
        lse_ref[:, hs] = lf_ref[h]


def _band_attn(z, batch, seq, group):
    dilation = B_GROUPS[group][1]
    assert B_GROUPS[group][0] // (2 * dilation) == BAND_HALF
    nt = seq // TILE

    def tile(blk, shift):
        return pl.BlockSpec((TILE, COL_BLK),
                            lambda b, t: (b * nt + jnp.clip(t + shift, 0, nt - 1), blk + group))

    out_spec = pl.BlockSpec((TILE, B_OUT_W), lambda b, t: (b * nt + t, 0))
    kern = functools.partial(_band_attn_kernel, dilation=dilation, length=seq // dilation,
                             chains=BAND_CHAINS[group])
    return pl.pallas_call(
        kern,
        grid=(batch, nt),
        in_specs=[tile(QB_BLK, 0), tile(KB_BLK, -1), tile(KB_BLK, 0), tile(KB_BLK, 1),
                  tile(VB_BLK, -1), tile(VB_BLK, 0), tile(VB_BLK, 1)],
        out_specs=[out_spec, out_spec],
        out_shape=[jax.ShapeDtypeStruct((batch * seq, B_OUT_W), BF16),
                   jax.ShapeDtypeStruct((batch * seq, B_OUT_W), F32)],
        scratch_shapes=[pltpu.VMEM((B_HEADS_PER_GROUP, TILE, B_HEAD_DIM), F32),
                        pltpu.VMEM((B_HEADS_PER_GROUP, TILE, B_HEAD_DIM), F32)],
        compiler_params=_params(2),
        name=f"band_attn_g{group}",
    )(z, z, z, z, z, z, z)


def _merge_out_kernel(x_ref, oa_ref, o0_ref, o1_ref, o2_ref, l0_ref, l1_ref, l2_ref, *rest):
    n_c = D_MODEL // COL_BLK
    ga_refs, gb_refs = rest[:n_c], rest[n_c:2 * n_c]
    wpa_ref, wpb_ref, wout_ref, y_ref = rest[2 * n_c:]
    l0, l1, l2 = l0_ref[...], l1_ref[...], l2_ref[...]
    mx = jnp.maximum(jnp.maximum(l0, l1), l2)
    e0, e1, e2 = jnp.exp(l0 - mx), jnp.exp(l1 - mx), jnp.exp(l2 - mx)
    num = (e0 * o0_ref[...].astype(F32) + e1 * o1_ref[...].astype(F32) + e2 * o2_ref[...].astype(F32))
    ob = (num / (e0 + e1 + e2)).astype(BF16)
    ya = jnp.dot(oa_ref[...], wpa_ref[...], preferred_element_type=F32)
    yb = jnp.dot(ob, wpb_ref[...], preferred_element_type=F32)
    merged = []
    for c in range(n_c):
        cs = slice(c * COL_BLK, (c + 1) * COL_BLK)
        merged.append((jax.nn.sigmoid(ga_refs[c][...].astype(F32)) * ya[:, cs]
                       + jax.nn.sigmoid(gb_refs[c][...].astype(F32)) * yb[:, cs]).astype(BF16))
    merged = jnp.concatenate(merged, axis=1)
    y_ref[...] = x_ref[...] + jnp.dot(merged, wout_ref[...], preferred_element_type=F32)


def _merge_out(x2d, oa, ob_groups, z, wpa, wpb, wout, tm):
    tokens = x2d.shape[0]
    row = lambda w: pl.BlockSpec((tm, w), lambda i: (i, 0))
    gate = lambda blk: pl.BlockSpec((tm, COL_BLK), lambda i: (i, blk))
    whole = lambda a: pl.BlockSpec(a.shape, lambda i: (0, 0), pipeline_mode=pl.Buffered(1))
    n_c = D_MODEL // COL_BLK
    (o0, l0), (o1, l1), (o2, l2) = ob_groups
    return pl.pallas_call(
        _merge_out_kernel,
        grid=(tokens // tm,),
        in_specs=[row(D_MODEL), row(A_HEADS * A_V_DIM)] + [row(B_OUT_W)] * 6
        + [gate(GA_BLK + c) for c in range(n_c)] + [gate(GB_BLK + c) for c in range(n_c)]
        + [whole(wpa), whole(wpb), whole(wout)],
        out_specs=row(D_MODEL),
        out_shape=jax.ShapeDtypeStruct((tokens, D_MODEL), F32),
        compiler_params=_params(1),
        name="merge_out",
    )(x2d, oa, o0, o1, o2, l0, l1, l2, *([z] * (2 * n_c)), wpa, wpb, wout)


def _ffn_kernel(x_ref, g_ref, w1_ref, w2_ref, gf_ref, y_ref, h_ref, acc_ref, *, final_norm):
    f = pl.program_id(1)

    @pl.when(f == 0)
    def _():
        x = x_ref[...]
        ms = jnp.mean(x * x, axis=-1, keepdims=True)
        h_ref[...] = (x * lax.rsqrt(ms + NORM_EPS) * g_ref[...]).astype(h_ref.dtype)
        acc_ref[...] = x

    u = jnp.dot(h_ref[...], w1_ref[...], preferred_element_type=F32)
    u = jnp.square(jnp.maximum(u, 0.0)).astype(BF16)
    acc_ref[...] += jnp.dot(u, w2_ref[...], preferred_element_type=F32)

    @pl.when(f == pl.num_programs(1) - 1)
    def _():
        y = acc_ref[...]
        if final_norm:
            ms = jnp.mean(y * y, axis=-1, keepdims=True)
            y = y * lax.rsqrt(ms + NORM_EPS) * gf_ref[...]
        y_ref[...] = y


def _ffn(x2d, g, w1, w2, g_final, final_norm, tm, tf):
    tokens = x2d.shape[0]
    kern = functools.partial(_ffn_kernel, final_norm=final_norm)
    return pl.pallas_call(
        kern,
        grid=(tokens // tm, D_FF // tf),
        in_specs=[
            pl.BlockSpec((tm, D_MODEL), lambda i, f: (i, 0)),
            pl.BlockSpec((1, D_MODEL), lambda i, f: (0, 0)),
            pl.BlockSpec((D_MODEL, tf), lambda i, f: (0, f)),
            pl.BlockSpec((tf, D_MODEL), lambda i, f: (f, 0)),
            pl.BlockSpec((1, D_MODEL), lambda i, f: (0, 0)),
        ],
        out_specs=pl.BlockSpec((tm, D_MODEL), lambda i, f: (i, 0)),
        out_shape=jax.ShapeDtypeStruct((tokens, D_MODEL), F32),
        scratch_shapes=[pltpu.VMEM((tm, D_MODEL), BF16), pltpu.VMEM((tm, D_MODEL), F32)],
        compiler_params=_params(2),
        name="ffn",
    )(x2d, g, w1, w2, g_final)


def _position_tables(seq):
    tabs_a = _rope_tables(seq, A_QK_DIM)
    tabs_b = tuple(jnp.stack([_sort_rows(t, dilation) for _, dilation in B_GROUPS])
                   for t in _rope_tables(seq, B_HEAD_DIM))
    return tabs_a, tabs_b


def _trunk(x, tabs, norm_mix, w_in, lambda_q1, lambda_k1, lambda_q2, lambda_k2, subln_g,
           w_proj_a, w_proj_b, w_out, norm_ffn, w1, w2, norm_final):
    batch, seq, _ = x.shape
    assert seq % TILE == 0
    depth = w_in.shape[0]
    x2d = x.reshape(batch * seq, D_MODEL)
    tabs_a, tabs_b = tabs
    row = lambda v: v.reshape(1, -1)
    for l in range(depth):
        lambda_init = 0.8 - 0.6 * math.exp(-0.3 * l)
        z = _in_proj(x2d, seq, row(norm_mix[l]), w_in[l], tabs_a, tabs_b)
        oa = _diff_attn(z, batch, seq, row(lambda_q1[l]), row(lambda_k1[l]), row(lambda_q2[l]),
                        row(lambda_k2[l]), row(subln_g[l]), lambda_init, tq=512, tk=1024)
        ob_groups = [_band_attn(z, batch, seq, g) for g in range(N_GROUPS)]
        x2d = _merge_out(x2d, oa, ob_groups, z, w_proj_a[l], w_proj_b[l], w_out[l], tm=256)
        x2d = _ffn(x2d, row(norm_ffn[l]), w1[l], w2[l], row(norm_final),
                   final_norm=(l == depth - 1), tm=512, tf=1024)
    return x2d.reshape(batch, seq, D_MODEL)


def kernel(x_prompt, x_sample, norm_mix, w_in, lambda_q1, lambda_k1, lambda_q2, lambda_k2, subln_g,
           w_proj_a, w_proj_b, w_out, norm_ffn, w1, w2, norm_final):
    w_in_blocks = _permute_rotary_columns(w_in.astype(BF16)).reshape(
        w_in.shape[0], D_MODEL, N_COL_BLKS, COL_BLK).transpose(0, 2, 1, 3)
    weights = (norm_mix, w_in_blocks, lambda_q1, lambda_k1, lambda_q2,
               lambda_k2, subln_g, w_proj_a.astype(BF16), w_proj_b.astype(BF16), w_out.astype(BF16), norm_ffn,
               w1.astype(BF16), w2.astype(BF16), norm_final)
    tabs = _position_tables(max(x_prompt.shape[1], x_sample.shape[1]))
    return _trunk(x_prompt, tabs, *weights), _trunk(x_sample, tabs, *weights)
```

```python
import functools
import math

import jax
import jax.numpy as jnp
import numpy as np
from jax import lax
from jax.experimental import pallas as pl
from jax.experimental.pallas import tpu as pltpu

D_MODEL = 2048
A_HEADS = 8
A_QK_DIM = 64
A_V_DIM = 2 * A_QK_DIM
B_GROUPS = ((128, 1), (512, 4), (2048, 16))
B_HEADS_PER_GROUP = 4
B_HEAD_DIM = 128
B_OUT_W = B_HEADS_PER_GROUP * B_HEAD_DIM
D_FF = 4 * D_MODEL
ROPE_THETA = 500000.0
ROPE_FRACTION_DEN = 4
NORM_EPS = 1e-6
SUBLN_EPS = 1e-5
NEG_BIG = -1e30
LOG2E = 1.4426950408889634

COL_BLK = 512
IN_WIDTH = 11776
N_COL_BLKS = IN_WIDTH // COL_BLK
QA_BLK, KA_BLK, VA_BLK = 0, 2, 4
QB_BLK, KB_BLK, VB_BLK = 6, 9, 12
GA_BLK, GB_BLK = 15, 19
N_GROUPS = len(B_GROUPS)
LANES = 128
TILE = 1024
BAND_HALF = 64
BAND_SUB = 128
KEY_PART = 256
BAND_CHAINS = (1, 1, 16)
VT_ROWS = A_V_DIM + 16

VMEM_LIMIT = 56 * 1024 * 1024

BF16 = jnp.bfloat16
F32 = jnp.float32


def _params(n_axes):
    return pltpu.CompilerParams(dimension_semantics=("arbitrary",) * n_axes,
                                vmem_limit_bytes=VMEM_LIMIT)


def _sort_rows(a, dilation):
    n, w = a.shape
    return a.reshape(n // TILE, TILE // dilation, dilation, w).transpose(0, 2, 1, 3).reshape(n, w)


def _rotary_layout(head_dim):
    half = head_dim // ROPE_FRACTION_DEN // 2
    heads = LANES // head_dim
    first = [h * head_dim + i for h in range(heads) for i in range(half)]
    second = [h * head_dim + half + i for h in range(heads) for i in range(half)]
    target1 = list(range(len(first)))
    target2 = list(range(LANES // 2, LANES // 2 + len(second)))
    src = np.arange(LANES)
    freq = np.full(LANES, -1)
    sign = np.zeros(LANES)
    for lanes, cols, sgn in ((target1, first, -1.0), (target2, second, 1.0)):
        for j, (lane, col) in enumerate(zip(lanes, cols)):
            src[lane], freq[lane], sign[lane] = col, j % half, sgn
    rotary, targets = set(first + second), set(target1 + target2)
    for lane, col in zip(sorted(rotary - targets), sorted(targets - rotary)):
        src[lane] = col
    assert sorted(src) == list(range(LANES))
    return src, freq, sign


def _runs(idx):
    out, start = [], 0
    for i in range(1, len(idx) + 1):
        if i == len(idx) or idx[i] != idx[i - 1] + 1:
            out.append((int(idx[start]), int(idx[i - 1]) + 1))
            start = i
    return out


def _permute_rotary_columns(w):
    def section(lo, hi, head_dim):
        ws = w[..., lo:hi].reshape(w.shape[:-1] + ((hi - lo) // LANES, LANES))
        src = _rotary_layout(head_dim)[0]
        ws = jnp.concatenate([ws[..., a:b] for a, b in _runs(src)], axis=-1)
        return ws.reshape(w.shape[:-1] + (hi - lo,))
    a_end, b_lo, b_end = VA_BLK * COL_BLK, QB_BLK * COL_BLK, VB_BLK * COL_BLK
    return jnp.concatenate([section(0, a_end, A_QK_DIM), w[..., a_end:b_lo],
                            section(b_lo, b_end, B_HEAD_DIM), w[..., b_end:]], axis=-1)


def _rope_tables(seq, head_dim):
    rot = head_dim // ROPE_FRACTION_DEN
    pos = jnp.arange(seq, dtype=F32)
    inv = ROPE_THETA ** (-jnp.arange(0, rot, 2, dtype=F32) / rot)
    _, freq, sign = _rotary_layout(head_dim)
    inv_lane = jnp.where(jnp.asarray(freq >= 0), inv[np.maximum(freq, 0)], 0.0)
    ang = pos[:, None] * inv_lane[None, :]
    return jnp.cos(ang), jnp.sin(ang) * jnp.asarray(sign, F32)[None, :]


def _rope_store(acc, cos, sin, scale, z_ref):
    for c in range(COL_BLK // LANES):
        xc = acc[:, c * LANES:(c + 1) * LANES]
        out = xc * cos + pltpu.roll(xc, LANES // 2, 1) * sin
        if scale != 1.0:
            out = out * scale
        z_ref[:, c * LANES:(c + 1) * LANES] = out.astype(z_ref.dtype)


def _row_order(j):
    return jnp.where((j >= QB_BLK) & (j < GA_BLK), (j - QB_BLK) % N_GROUPS, 0)


def _in_proj_kernel(x_ref, g_ref, w_ref, ca_ref, sa_ref, cb_ref, sb_ref, z_ref, h_ref, slab_ref):
    j = pl.program_id(1)

    @pl.when(j == 0)
    def _():
        x = x_ref[...]
        rinv = lax.rsqrt(jnp.mean(x * x, axis=-1, keepdims=True) + NORM_EPS)
        for c in range(D_MODEL // LANES):
            cs = slice(c * LANES, (c + 1) * LANES)
            slab = x_ref[:, cs] * rinv * g_ref[:, cs]
            h_ref[0, :, cs] = slab.astype(h_ref.dtype)
            slab_ref[...] = slab
            for order in range(1, N_GROUPS):
                dilation = B_GROUPS[order][1]
                per = TILE // dilation
                for rho in range(dilation):
                    h_ref[order, rho * per:(rho + 1) * per, cs] = (
                        slab_ref[pl.ds(rho, per, stride=dilation), :].astype(h_ref.dtype))

    def project():
        return jnp.dot(h_ref[_row_order(j)], w_ref[...], preferred_element_type=F32)

    @pl.when(j < KA_BLK)
    def _():
        _rope_store(project(), ca_ref[...], sa_ref[...], A_QK_DIM ** -0.5 * LOG2E, z_ref)

    @pl.when((j >= KA_BLK) & (j < VA_BLK))
    def _():
        _rope_store(project(), ca_ref[...], sa_ref[...], 1.0, z_ref)

    @pl.when((j >= QB_BLK) & (j < KB_BLK))
    def _():
        _rope_store(project(), cb_ref[...], sb_ref[...], B_HEAD_DIM ** -0.5, z_ref)

    @pl.when((j >= KB_BLK) & (j < VB_BLK))
    def _():
        _rope_store(project(), cb_ref[...], sb_ref[...], 1.0, z_ref)

    @pl.when(((j >= VA_BLK) & (j < QB_BLK)) | (j >= VB_BLK))
    def _():
        z_ref[...] = project().astype(z_ref.dtype)


def _in_proj(x2d, seq, g, w_bf16, tabs_a, tabs_b):
    tokens = x2d.shape[0]
    nt = seq // TILE
    tab_a = pl.BlockSpec((TILE, LANES), lambda i, j: (i % nt, 0))
    tab_b = pl.BlockSpec((None, TILE, LANES), lambda i, j: (_row_order(j), i % nt, 0))
    return pl.pallas_call(
        _in_proj_kernel,
        grid=(tokens // TILE, N_COL_BLKS),
        in_specs=[
            pl.BlockSpec((TILE, D_MODEL), lambda i, j: (i, 0)),
            pl.BlockSpec((1, D_MODEL), lambda i, j: (0, 0)),
            pl.BlockSpec((None, D_MODEL, COL_BLK), lambda i, j: (j, 0, 0)),
            tab_a, tab_a, tab_b, tab_b,
        ],
        out_specs=pl.BlockSpec((TILE, COL_BLK), lambda i, j: (i, j)),
        out_shape=jax.ShapeDtypeStruct((tokens, IN_WIDTH), BF16),
        scratch_shapes=[pltpu.VMEM((N_GROUPS, TILE, D_MODEL), BF16), pltpu.VMEM((TILE, LANES), F32)],
        compiler_params=_params(2),
        name="in_proj",
    )(x2d, g, w_bf16, *tabs_a, *tabs_b)


_MAP0_LANES = _runs(np.flatnonzero(_rotary_layout(A_QK_DIM)[0] < A_QK_DIM))


def _diff_attn_kernel(q_ref, k_ref, v_ref, lq1_ref, lk1_ref, lq2_ref, lk2_ref, g_ref, o_ref,
                      q2t_ref, vt_ref, m_ref, acc_ref, s_ref, cmax_ref, *, tq, tk, seq, lambda_init):
    qi = pl.program_id(2)
    nck = seq // tk

    @pl.when(qi == 0)
    def _():
        def transpose_chunk(c, carry):
            rows = pl.ds(pl.multiple_of(c * tk, tk), tk)
            vt_ref[c, :A_V_DIM, :] = v_ref[rows, :].astype(F32).T.astype(BF16)
            sub = lax.broadcasted_iota(jnp.int32, (VT_ROWS - A_V_DIM, tk), 0)
            vt_ref[c, A_V_DIM:, :] = jnp.where(sub == 0, 1.0, 0.0).astype(BF16)
            return carry
        lax.fori_loop(0, nck, transpose_chunk, 0)

    qt = q_ref[...].astype(F32).T
    dim = lax.broadcasted_iota(jnp.int32, (A_V_DIM, 1), 0)
    map0 = functools.reduce(jnp.logical_or, [(dim >= a) & (dim < b) for a, b in _MAP0_LANES])
    q2t_ref[:, :tq] = jnp.where(map0, qt, 0.0).astype(BF16)
    q2t_ref[:, tq:] = jnp.where(map0, 0.0, qt).astype(BF16)
    m_ref[...] = jnp.full(m_ref.shape, NEG_BIG, F32)
    acc_ref[...] = jnp.zeros(acc_ref.shape, F32)

    nparts = tk // KEY_PART

    def scores_part(c, slot, j):
        rows = pl.ds(pl.multiple_of(c * tk, tk) + j * KEY_PART, KEY_PART)
        s = jnp.dot(k_ref[rows, :], q2t_ref[...], preferred_element_type=F32)
        s_ref[slot, j * KEY_PART:(j + 1) * KEY_PART, :] = s
        return jnp.max(s, axis=0, keepdims=True)

    def scores(c, slot, do):
        cmax = None
        for j in range(nparts):
            part_max = scores_part(c, slot, j)
            cmax = part_max if cmax is None else jnp.maximum(cmax, part_max)
            do(j)
        cmax_ref[slot] = jnp.broadcast_to(cmax, cmax_ref.shape[1:])

    def step(c, slot, with_next):
        m_old = m_ref[0:1, :]
        m_new = jnp.maximum(m_old, cmax_ref[slot, 0:1, :])
        acc = [jnp.exp2(m_old - m_new) * acc_ref[...]]

        def pv_part(j):
            keys = slice(j * KEY_PART, (j + 1) * KEY_PART)
            p = jnp.exp2(s_ref[slot, keys, :] - m_new).astype(BF16)
            acc[0] = acc[0] + jnp.dot(vt_ref[c, :, keys], p, preferred_element_type=F32)

        if with_next:
            scores(c + 1, 1 - slot, pv_part)
        else:
            for j in range(nparts):
                pv_part(j)
        acc_ref[...] = acc[0]
        m_ref[...] = jnp.broadcast_to(m_new, m_ref.shape)

    scores(0, 0, lambda j: None)
    group = 4 if nck % 4 == 0 else 2

    def trip(c0, last):
        for u in range(group):
            step(c0 + u, u % 2, not (last and u == group - 1))

    def body(i, carry):
        trip(group * i, False)
        return carry

    lax.fori_loop(0, nck // group - 1, body, 0)
    trip(nck - group, True)

    acc = acc_ref[...]
    ot = acc[:A_V_DIM] / acc[A_V_DIM:A_V_DIM + 1]
    lam = (jnp.exp(jnp.sum(lq1_ref[...] * lk1_ref[...], axis=-1, keepdims=True))
           - jnp.exp(jnp.sum(lq2_ref[...] * lk2_ref[...], axis=-1, keepdims=True)) + lambda_init)
    o = ot[:, :tq] - lam * ot[:, tq:]
    ms = jnp.mean(o * o, axis=0, keepdims=True)
    o = o * lax.rsqrt(ms + SUBLN_EPS)
    o_ref[...] = (o.T * g_ref[...] * (1.0 - lambda_init)).astype(o_ref.dtype)


def _diff_attn(z, batch, seq, lq1, lk1, lq2, lk2, subln_g, lambda_init, tq, tk):
    tokens = batch * seq
    nq = seq // tq
    assert seq % (2 * tk) == 0
    vec = lambda n: pl.BlockSpec((1, n), lambda b, h, qi: (0, 0))
    kern = functools.partial(_diff_attn_kernel, tq=tq, tk=tk, seq=seq, lambda_init=lambda_init)
    return pl.pallas_call(
        kern,
        grid=(batch, A_HEADS, nq),
        in_specs=[
            pl.BlockSpec((tq, A_V_DIM), lambda b, h, qi: (b * nq + qi, h)),
            pl.BlockSpec((seq, A_V_DIM), lambda b, h, qi: (b, A_HEADS + h)),
            pl.BlockSpec((seq, A_V_DIM), lambda b, h, qi: (b, 2 * A_HEADS + h)),
            vec(A_QK_DIM), vec(A_QK_DIM), vec(A_QK_DIM), vec(A_QK_DIM), vec(A_V_DIM),
        ],
        out_specs=pl.BlockSpec((tq, A_V_DIM), lambda b, h, qi: (b * nq + qi, h)),
        out_shape=jax.ShapeDtypeStruct((tokens, A_HEADS * A_V_DIM), BF16),
        scratch_shapes=[
            pltpu.VMEM((A_V_DIM, 2 * tq), BF16),
            pltpu.VMEM((seq // tk, VT_ROWS, tk), BF16),
            pltpu.VMEM((8, 2 * tq), F32),
            pltpu.VMEM((VT_ROWS, 2 * tq), F32),
            pltpu.VMEM((2, tk, 2 * tq), F32),
            pltpu.VMEM((2, 8, 2 * tq), F32),
        ],
        compiler_params=_params(3),
        name="diff_attn",
    )(z, z, z, lq1, lk1, lq2, lk2, subln_g)


def _band_window(prev_ref, cur_ref, next_ref, base, lo, hi, per, hs):
    parts = []
    if lo < 0:
        parts.append(prev_ref[base + per + lo:base + per + min(hi, 0), hs])
    if hi > 0 and lo < per:
        parts.append(cur_ref[base + max(lo, 0):base + min(hi, per), hs])
    if hi > per:
        parts.append(next_ref[base + max(lo, per) - per:base + hi - per, hs])
    return parts[0] if len(parts) == 1 else jnp.concatenate(parts, axis=0)


def _band_attn_kernel(q_ref, kp_ref, kc_ref, kn_ref, vp_ref, vc_ref, vn_ref, o_ref, lse_ref, of_ref, lf_ref,
                      *, dilation, length, chains):
    t = pl.program_id(1)
    per = TILE // dilation
    sub = min(per, BAND_SUB)
    ncls = BAND_SUB // sub
    nkeys = sub + 2 * BAND_HALF
    shape = (ncls * sub, ncls * nkeys)
    rows = lax.broadcasted_iota(jnp.int32, shape, 0)
    cols = lax.broadcasted_iota(jnp.int32, shape, 1)
    row_cls, row_pos = rows // sub, rows % sub
    col_cls, col_pos = cols // nkeys, cols % nkeys
    in_band = (jnp.abs(col_pos - BAND_HALF - row_pos) <= BAND_HALF) & (row_cls == col_cls)

    def window(refs, classes, lo, hi, hs):
        parts = [_band_window(*refs, rho * per, lo, hi, per, hs) for rho in classes]
        return parts[0] if len(parts) == 1 else jnp.concatenate(parts, axis=0)

    for i in range(per // sub):
        kpos = t * per + i * sub - BAND_HALF + col_pos
        mask = in_band & (kpos >= 0) & (kpos < length)
        lo, hi = i * sub - BAND_HALF, i * sub + sub + BAND_HALF
        tiles = [(range(rho0, rho0 + ncls), h) for rho0 in range(0, dilation, ncls)
                 for h in range(B_HEADS_PER_GROUP)]
        for b0 in range(0, len(tiles), chains):
            batch = tiles[b0:b0 + chains]
            scores = []
            for classes, h in batch:
                hs = slice(h * B_HEAD_DIM, (h + 1) * B_HEAD_DIM)
                q = q_ref[classes[0] * per + i * sub:classes[0] * per + i * sub + ncls * sub, hs]
                k = window((kp_ref, kc_ref, kn_ref), classes, lo, hi, hs)
                scores.append(lax.dot_general(q, k, (((1,), (1,)), ((), ())), preferred_element_type=F32))
            stats = []
            for s in scores:
                s = jnp.where(mask, s, NEG_BIG)
                m = jnp.max(s, axis=-1, keepdims=True)
                p = jnp.exp(s - m)
                stats.append((m, jnp.sum(p, axis=-1, keepdims=True), p.astype(BF16)))
            for (classes, h), (m, l, p) in zip(batch, stats):
                hs = slice(h * B_HEAD_DIM, (h + 1) * B_HEAD_DIM)
                v = window((vp_ref, vc_ref, vn_ref), classes, lo, hi, hs)
                o = jnp.dot(p, v, preferred_element_type=F32) / l
                lse = jnp.broadcast_to(m + jnp.log(l), (ncls * sub, B_HEAD_DIM))
                for n, rho in enumerate(classes):
                    out_rows = pl.ds(i * sub * dilation + rho, sub, stride=dilation) if dilation > 1 \
                        else pl.ds(i * sub, sub)
                    of_ref[h, out_rows, :] = o[n * sub:(n + 1) * sub]
                    lf_ref[h, out_rows, :] = lse[n * sub:(n + 1) * sub]
    for h in range(B_HEADS_PER_GROUP):
        hs = slice(h * B_HEAD_DIM, (h + 1) * B_HEAD_DIM)
        o_ref[:, hs] = of_ref[h].astype(o_ref.dtype)
        lse_ref[:, hs] = lf_ref[h]


def _band_attn(z, batch, seq, group):
    dilation = B_GROUPS[group][1]
    assert B_GROUPS[group][0] // (2 * dilation) == BAND_HALF
    nt = seq // TILE

    def tile(blk, shift):
        return pl.BlockSpec((TILE, COL_BLK),
                            lambda b, t: (b * nt + jnp.clip(t + shift, 0, nt - 1), blk + group))

    out_spec = pl.BlockSpec((TILE, B_OUT_W), lambda b, t: (b * nt + t, 0))
    kern = functools.partial(_band_attn_kernel, dilation=dilation, length=seq // dilation,
                             chains=BAND_CHAINS[group])
    return pl.pallas_call(
        kern,
        grid=(batch, nt),
        in_specs=[tile(QB_BLK, 0), tile(KB_BLK, -1), tile(KB_BLK, 0), tile(KB_BLK, 1),
                  tile(VB_BLK, -1), tile(VB_BLK, 0), tile(VB_BLK, 1)],
        out_specs=[out_spec, out_spec],
        out_shape=[jax.ShapeDtypeStruct((batch * seq, B_OUT_W), BF16),
                   jax.ShapeDtypeStruct((batch * seq, B_OUT_W), F32)],
        scratch_shapes=[pltpu.VMEM((B_HEADS_PER_GROUP, TILE, B_HEAD_DIM), F32),
                        pltpu.VMEM((B_HEADS_PER_GROUP, TILE, B_HEAD_DIM), F32)],
        compiler_params=_params(2),
        name=f"band_attn_g{group}",
    )(z, z, z, z, z, z, z)


def _merge_out_kernel(x_ref, oa_ref, o0_ref, o1_ref, o2_ref, l0_ref, l1_ref, l2_ref, *rest):
    n_c = D_MODEL // COL_BLK
    ga_refs, gb_refs = rest[:n_c], rest[n_c:2 * n_c]
    wpa_ref, wpb_ref, wout_ref, y_ref = rest[2 * n_c:]
    l0, l1, l2 = l0_ref[...], l1_ref[...], l2_ref[...]
    mx = jnp.maximum(jnp.maximum(l0, l1), l2)
    e0, e1, e2 = jnp.exp(l0 - mx), jnp.exp(l1 - mx), jnp.exp(l2 - mx)
    num = (e0 * o0_ref[...].astype(F32) + e1 * o1_ref[...].astype(F32) + e2 * o2_ref[...].astype(F32))
    ob = (num / (e0 + e1 + e2)).astype(BF16)
    ya = jnp.dot(oa_ref[...], wpa_ref[...], preferred_element_type=F32)
    yb = jnp.dot(ob, wpb_ref[...], preferred_element_type=F32)
    merged = []
    for c in range(n_c):
        cs = slice(c * COL_BLK, (c + 1) * COL_BLK)
        merged.append((jax.nn.sigmoid(ga_refs[c][...].astype(F32)) * ya[:, cs]
                       + jax.nn.sigmoid(gb_refs[c][...].astype(F32)) * yb[:, cs]).astype(BF16))
    merged = jnp.concatenate(merged, axis=1)
    y_ref[...] = x_ref[...] + jnp.dot(merged, wout_ref[...], preferred_element_type=F32)


def _merge_out(x2d, oa, ob_groups, z, wpa, wpb, wout, tm):
    tokens = x2d.shape[0]
    row = lambda w: pl.BlockSpec((tm, w), lambda i: (i, 0))
    gate = lambda blk: pl.BlockSpec((tm, COL_BLK), lambda i: (i, blk))
    whole = lambda a: pl.BlockSpec(a.shape, lambda i: (0, 0), pipeline_mode=pl.Buffered(1))
    n_c = D_MODEL // COL_BLK
    (o0, l0), (o1, l1), (o2, l2) = ob_groups
    return pl.pallas_call(
        _merge_out_kernel,
        grid=(tokens // tm,),
        in_specs=[row(D_MODEL), row(A_HEADS * A_V_DIM)] + [row(B_OUT_W)] * 6
        + [gate(GA_BLK + c) for c in range(n_c)] + [gate(GB_BLK + c) for c in range(n_c)]
        + [whole(wpa), whole(wpb), whole(wout)],
        out_specs=row(D_MODEL),
        out_shape=jax.ShapeDtypeStruct((tokens, D_MODEL), F32),
        compiler_params=_params(1),
        name="merge_out",
    )(x2d, oa, o0, o1, o2, l0, l1, l2, *([z] * (2 * n_c)), wpa, wpb, wout)


def _ffn_kernel(x_ref, g_ref, w1_ref, w2_ref, gf_ref, y_ref, h_ref, acc_ref, *, final_norm):
    f = pl.program_id(1)

    @pl.when(f == 0)
    def _():
        x = x_ref[...]
        ms = jnp.mean(x * x, axis=-1, keepdims=True)
        h_ref[...] = (x * lax.rsqrt(ms + NORM_EPS) * g_ref[...]).astype(h_ref.dtype)
        acc_ref[...] = x

    u = jnp.dot(h_ref[...], w1_ref[...], preferred_element_type=F32)
    u = jnp.square(jnp.maximum(u, 0.0)).astype(BF16)
    acc_ref[...] += jnp.dot(u, w2_ref[...], preferred_element_type=F32)

    @pl.when(f == pl.num_programs(1) - 1)
    def _():
        y = acc_ref[...]
        if final_norm:
            ms = jnp.mean(y * y, axis=-1, keepdims=True)
            y = y * lax.rsqrt(ms + NORM_EPS) * gf_ref[...]
        y_ref[...] = y


def _ffn(x2d, g, w1, w2, g_final, final_norm, tm, tf):
    tokens = x2d.shape[0]
    kern = functools.partial(_ffn_kernel, final_norm=final_norm)
    return pl.pallas_call(
        kern,
        grid=(tokens // tm, D_FF // tf),
        in_specs=[
            pl.BlockSpec((tm, D_MODEL), lambda i, f: (i, 0)),
            pl.BlockSpec((1, D_MODEL), lambda i, f: (0, 0)),
            pl.BlockSpec((D_MODEL, tf), lambda i, f: (0, f)),
            pl.BlockSpec((tf, D_MODEL), lambda i, f: (f, 0)),
            pl.BlockSpec((1, D_MODEL), lambda i, f: (0, 0)),
        ],
        out_specs=pl.BlockSpec((tm, D_MODEL), lambda i, f: (i, 0)),
        out_shape=jax.ShapeDtypeStruct((tokens, D_MODEL), F32),
        scratch_shapes=[pltpu.VMEM((tm, D_MODEL), BF16), pltpu.VMEM((tm, D_MODEL), F32)],
        compiler_params=_params(2),
        name="ffn",
    )(x2d, g, w1, w2, g_final)


def _position_tables(seq):
    tabs_a = _rope_tables(seq, A_QK_DIM)
    tabs_b = tuple(jnp.stack([_sort_rows(t, dilation) for _, dilation in B_GROUPS])
                   for t in _rope_tables(seq, B_HEAD_DIM))
    return tabs_a, tabs_b


def _trunk(x, tabs, norm_mix, w_in, lambda_q1, lambda_k1, lambda_q2, lambda_k2, subln_g,
           w_proj_a, w_proj_b, w_out, norm_ffn, w1, w2, norm_final):
    batch, seq, _ = x.shape
    assert seq % TILE == 0
    depth = w_in.shape[0]
    x2d = x.reshape(batch * seq, D_MODEL)
    tabs_a, tabs_b = tabs
    row = lambda v: v.reshape(1, -1)
    for l in range(depth):
        lambda_init = 0.8 - 0.6 * math.exp(-0.3 * l)
        z = _in_proj(x2d, seq, row(norm_mix[l]), w_in[l], tabs_a, tabs_b)
        oa = _diff_attn(z, batch, seq, row(lambda_q1[l]), row(lambda_k1[l]), row(lambda_q2[l]),
                        row(lambda_k2[l]), row(subln_g[l]), lambda_init, tq=512, tk=1024)
        ob_groups = [_band_attn(z, batch, seq, g) for g in range(N_GROUPS)]
        x2d = _merge_out(x2d, oa, ob_groups, z, w_proj_a[l], w_proj_b[l], w_out[l], tm=256)
        x2d = _ffn(x2d, row(norm_ffn[l]), w1[l], w2[l], row(norm_final),
                   final_norm=(l == depth - 1), tm=512, tf=1024)
    return x2d.reshape(batch, seq, D_MODEL)


def kernel(x_prompt, x_sample, norm_mix, w_in, lambda_q1, lambda_k1, lambda_q2, lambda_k2, subln_g,
           w_proj_a, w_proj_b, w_out, norm_ffn, w1, w2, norm_final):
    w_in_blocks = _permute_rotary_columns(w_in.astype(BF16)).reshape(
        w_in.shape[0], D_MODEL, N_COL_BLKS, COL_BLK).transpose(0, 2, 1, 3)
    weights = (norm_mix, w_in_blocks, lambda_q1, lambda_k1, lambda_q2,
               lambda_k2, subln_g, w_proj_a.astype(BF16), w_proj_b.astype(BF16), w_out.astype(BF16), norm_ffn,
               w1.astype(BF16), w2.astype(BF16), norm_final)
    tabs = _position_tables(max(x_prompt.shape[1], x_sample.shape[1]))
    return _trunk(x_prompt, tabs, *weights), _trunk(x_sample, tabs, *weights)
```

```python
import functools
import math

import jax
import jax.numpy as jnp
import numpy as np
from jax import lax
from jax.experimental import pallas as pl
from jax.experimental.pallas import tpu as pltpu

D_MODEL = 2048
A_HEADS = 8
A_QK_DIM = 64
A_V_DIM = 2 * A_QK_DIM
B_GROUPS = ((128, 1), (512, 4), (2048, 16))
B_HEADS_PER_GROUP = 4
B_HEAD_DIM = 128
B_OUT_W = B_HEADS_PER_GROUP * B_HEAD_DIM
D_FF = 4 * D_MODEL
ROPE_THETA = 500000.0
ROPE_FRACTION_DEN = 4
NORM_EPS = 1e-6
SUBLN_EPS = 1e-5
NEG_BIG = -1e30
LOG2E = 1.4426950408889634

COL_BLK = 512
IN_WIDTH = 11776
N_COL_BLKS = IN_WIDTH // COL_BLK
QA_BLK, KA_BLK, VA_BLK = 0, 2, 4
QB_BLK, KB_BLK, VB_BLK = 6, 9, 12
GA_BLK, GB_BLK = 15, 19
N_GROUPS = len(B_GROUPS)
LANES = 128
TILE = 1024
BAND_HALF = 64
BAND_SUB = 128
KEY_PART = 256
BAND_CHAINS = (1, 1, 16)
VT_ROWS = A_V_DIM + 16

VMEM_LIMIT = 56 * 1024 * 1024

BF16 = jnp.bfloat16
F32 = jnp.float32


def _params(n_axes):
    return pltpu.CompilerParams(dimension_semantics=("arbitrary",) * n_axes,
                                vmem_limit_bytes=VMEM_LIMIT)


def _rotary_layout(head_dim):
    half = head_dim // ROPE_FRACTION_DEN // 2
    heads = LANES // head_dim
    first = [h * head_dim + i for h in range(heads) for i in range(half)]
    second = [h * head_dim + half + i for h in range(heads) for i in range(half)]
    target1 = list(range(len(first)))
    target2 = list(range(LANES // 2, LANES // 2 + len(second)))
    src = np.arange(LANES)
    freq = np.full(LANES, -1)
    sign = np.zeros(LANES)
    for lanes, cols, sgn in ((target1, first, -1.0), (target2, second, 1.0)):
        for j, (lane, col) in enumerate(zip(lanes, cols)):
            src[lane], freq[lane], sign[lane] = col, j % half, sgn
    rotary, targets = set(first + second), set(target1 + target2)
    for lane, col in zip(sorted(rotary - targets), sorted(targets - rotary)):
        src[lane] = col
    assert sorted(src) == list(range(LANES))
    return src, freq, sign


def _runs(idx):
    out, start = [], 0
    for i in range(1, len(idx) + 1):
        if i == len(idx) or idx[i] != idx[i - 1] + 1:
            out.append((int(idx[start]), int(idx[i - 1]) + 1))
            start = i
    return out


def _permute_rotary_columns(w):
    def section(lo, hi, head_dim):
        ws = w[..., lo:hi].reshape(w.shape[:-1] + ((hi - lo) // LANES, LANES))
        src = _rotary_layout(head_dim)[0]
        ws = jnp.concatenate([ws[..., a:b] for a, b in _runs(src)], axis=-1)
        return ws.reshape(w.shape[:-1] + (hi - lo,))
    a_end, b_lo, b_end = VA_BLK * COL_BLK, QB_BLK * COL_BLK, VB_BLK * COL_BLK
    return jnp.concatenate([section(0, a_end, A_QK_DIM), w[..., a_end:b_lo],
                            section(b_lo, b_end, B_HEAD_DIM), w[..., b_end:]], axis=-1)


def _rope_tables(seq, head_dim, dilations):
    rot = head_dim // ROPE_FRACTION_DEN
    inv = ROPE_THETA ** (-jnp.arange(0, rot, 2, dtype=F32) / rot)
    _, freq, sign = _rotary_layout(head_dim)
    inv_lane = jnp.where(jnp.asarray(freq >= 0), inv[np.maximum(freq, 0)], 0.0)[None, :]
    sign = jnp.asarray(sign, F32)[None, :]
    base = jnp.arange(0, seq, TILE, dtype=F32)[:, None] * inv_lane
    cos_a, sin_a = jnp.cos(base)[:, None, :], jnp.sin(base)[:, None, :]
    cos_tabs, sin_tabs = [], []
    for dilation in dilations:
        offset = np.arange(TILE).reshape(TILE // dilation, dilation).T.reshape(TILE)
        ang = jnp.asarray(offset, F32)[:, None] * inv_lane
        cos_b, sin_b = jnp.cos(ang)[None], jnp.sin(ang)[None]
        cos_tabs.append((cos_a * cos_b - sin_a * sin_b).reshape(seq, LANES))
        sin_tabs.append(((sin_a * cos_b + cos_a * sin_b) * sign).reshape(seq, LANES))
    return jnp.stack(cos_tabs), jnp.stack(sin_tabs)


def _rope_store(acc, cos, sin, scale, z_ref):
    for c in range(COL_BLK // LANES):
        xc = acc[:, c * LANES:(c + 1) * LANES]
        out = xc * cos + pltpu.roll(xc, LANES // 2, 1) * sin
        if scale != 1.0:
            out = out * scale
        z_ref[:, c * LANES:(c + 1) * LANES] = out.astype(z_ref.dtype)


def _row_order(j):
    return jnp.where((j >= QB_BLK) & (j < GA_BLK), (j - QB_BLK) % N_GROUPS, 0)


def _in_proj_kernel(x_ref, g_ref, w_ref, ca_ref, sa_ref, cb_ref, sb_ref, z_ref, h_ref, slab_ref):
    j = pl.program_id(1)

    @pl.when(j == 0)
    def _():
        x = x_ref[...]
        rinv = lax.rsqrt(jnp.mean(x * x, axis=-1, keepdims=True) + NORM_EPS)
        for c in range(D_MODEL // LANES):
            cs = slice(c * LANES, (c + 1) * LANES)
            slab = x_ref[:, cs] * rinv * g_ref[:, cs]
            h_ref[0, :, cs] = slab.astype(h_ref.dtype)
            slab_ref[...] = slab
            for order in range(1, N_GROUPS):
                dilation = B_GROUPS[order][1]
                per = TILE // dilation
                for rho in range(dilation):
                    h_ref[order, rho * per:(rho + 1) * per, cs] = (
                        slab_ref[pl.ds(rho, per, stride=dilation), :].astype(h_ref.dtype))

    def project():
        return jnp.dot(h_ref[_row_order(j)], w_ref[...], preferred_element_type=F32)

    @pl.when(j < KA_BLK)
    def _():
        _rope_store(project(), ca_ref[...], sa_ref[...], A_QK_DIM ** -0.5 * LOG2E, z_ref)

    @pl.when((j >= KA_BLK) & (j < VA_BLK))
    def _():
        _rope_store(project(), ca_ref[...], sa_ref[...], 1.0, z_ref)

    @pl.when((j >= QB_BLK) & (j < KB_BLK))
    def _():
        _rope_store(project(), cb_ref[...], sb_ref[...], B_HEAD_DIM ** -0.5, z_ref)

    @pl.when((j >= KB_BLK) & (j < VB_BLK))
    def _():
        _rope_store(project(), cb_ref[...], sb_ref[...], 1.0, z_ref)

    @pl.when(((j >= VA_BLK) & (j < QB_BLK)) | (j >= VB_BLK))
    def _():
        z_ref[...] = project().astype(z_ref.dtype)


def _in_proj(x2d, seq, g, w_bf16, tabs_a, tabs_b):
    tokens = x2d.shape[0]
    nt = seq // TILE
    tab_a = pl.BlockSpec((TILE, LANES), lambda i, j: (i % nt, 0))
    tab_b = pl.BlockSpec((None, TILE, LANES), lambda i, j: (_row_order(j), i % nt, 0))
    return pl.pallas_call(
        _in_proj_kernel,
        grid=(tokens // TILE, N_COL_BLKS),
        in_specs=[
            pl.BlockSpec((TILE, D_MODEL), lambda i, j: (i, 0)),
            pl.BlockSpec((1, D_MODEL), lambda i, j: (0, 0)),
            pl.BlockSpec((None, D_MODEL, COL_BLK), lambda i, j: (j, 0, 0)),
            tab_a, tab_a, tab_b, tab_b,
        ],
        out_specs=pl.BlockSpec((TILE, COL_BLK), lambda i, j: (i, j)),
        out_shape=jax.ShapeDtypeStruct((tokens, IN_WIDTH), BF16),
        scratch_shapes=[pltpu.VMEM((N_GROUPS, TILE, D_MODEL), BF16), pltpu.VMEM((TILE, LANES), F32)],
        compiler_params=_params(2),
        name="in_proj",
    )(x2d, g, w_bf16, *tabs_a, *tabs_b)


_MAP0_LANES = _runs(np.flatnonzero(_rotary_layout(A_QK_DIM)[0] < A_QK_DIM))


def _diff_attn_kernel(q_ref, k_ref, v_ref, lq1_ref, lk1_ref, lq2_ref, lk2_ref, g_ref, o_ref,
                      q2t_ref, vt_ref, m_ref, acc_ref, s_ref, cmax_ref, *, tq, tk, seq, lambda_init):
    qi = pl.program_id(2)
    nck = seq // tk

    @pl.when(qi == 0)
    def _():
        def transpose_chunk(c, carry):
            rows = pl.ds(pl.multiple_of(c * tk, tk), tk)
            vt_ref[c, :A_V_DIM, :] = v_ref[rows, :].astype(F32).T.astype(BF16)
            sub = lax.broadcasted_iota(jnp.int32, (VT_ROWS - A_V_DIM, tk), 0)
            vt_ref[c, A_V_DIM:, :] = jnp.where(sub == 0, 1.0, 0.0).astype(BF16)
            return carry
        lax.fori_loop(0, nck, transpose_chunk, 0)

    qt = q_ref[...].astype(F32).T
    dim = lax.broadcasted_iota(jnp.int32, (A_V_DIM, 1), 0)
    map0 = functools.reduce(jnp.logical_or, [(dim >= a) & (dim < b) for a, b in _MAP0_LANES])
    q2t_ref[:, :tq] = jnp.where(map0, qt, 0.0).astype(BF16)
    q2t_ref[:, tq:] = jnp.where(map0, 0.0, qt).astype(BF16)
    m_ref[...] = jnp.full(m_ref.shape, NEG_BIG, F32)
    acc_ref[...] = jnp.zeros(acc_ref.shape, F32)

    nparts = tk // KEY_PART

    def scores_part(c, slot, j):
        rows = pl.ds(pl.multiple_of(c * tk, tk) + j * KEY_PART, KEY_PART)
        s = jnp.dot(k_ref[rows, :], q2t_ref[...], preferred_element_type=F32)
        s_ref[slot, j * KEY_PART:(j + 1) * KEY_PART, :] = s
        return jnp.max(s, axis=0, keepdims=True)

    def scores(c, slot, do):
        cmax = None
        for j in range(nparts):
            part_max = scores_part(c, slot, j)
            cmax = part_max if cmax is None else jnp.maximum(cmax, part_max)
            do(j)
        cmax_ref[slot] = jnp.broadcast_to(cmax, cmax_ref.shape[1:])

    def step(c, slot, with_next):
        m_old = m_ref[0:1, :]
        m_new = jnp.maximum(m_old, cmax_ref[slot, 0:1, :])
        acc = [jnp.exp2(m_old - m_new) * acc_ref[...]]

        def pv_part(j):
            keys = slice(j * KEY_PART, (j + 1) * KEY_PART)
            p = jnp.exp2(s_ref[slot, keys, :] - m_new).astype(BF16)
            acc[0] = acc[0] + jnp.dot(vt_ref[c, :, keys], p, preferred_element_type=F32)

        if with_next:
            scores(c + 1, 1 - slot, pv_part)
        else:
            for j in range(nparts):
                pv_part(j)
        acc_ref[...] = acc[0]
        m_ref[...] = jnp.broadcast_to(m_new, m_ref.shape)

    scores(0, 0, lambda j: None)
    group = 4 if nck % 4 == 0 else 2

    def trip(c0, last):
        for u in range(group):
            step(c0 + u, u % 2, not (last and u == group - 1))

    def body(i, carry):
        trip(group * i, False)
        return carry

    lax.fori_loop(0, nck // group - 1, body, 0)
    trip(nck - group, True)

    acc = acc_ref[...]
    ot = acc[:A_V_DIM] / acc[A_V_DIM:A_V_DIM + 1]
    lam = (jnp.exp(jnp.sum(lq1_ref[...] * lk1_ref[...], axis=-1, keepdims=True))
           - jnp.exp(jnp.sum(lq2_ref[...] * lk2_ref[...], axis=-1, keepdims=True)) + lambda_init)
    o = ot[:, :tq] - lam * ot[:, tq:]
    ms = jnp.mean(o * o, axis=0, keepdims=True)
    o = o * lax.rsqrt(ms + SUBLN_EPS)
    o_ref[...] = (o.T * g_ref[...] * (1.0 - lambda_init)).astype(o_ref.dtype)


def _diff_attn(z, batch, seq, lq1, lk1, lq2, lk2, subln_g, lambda_init, tq, tk):
    tokens = batch * seq
    nq = seq // tq
    assert seq % (2 * tk) == 0
    vec = lambda n: pl.BlockSpec((1, n), lambda b, h, qi: (0, 0))
    kern = functools.partial(_diff_attn_kernel, tq=tq, tk=tk, seq=seq, lambda_init=lambda_init)
    return pl.pallas_call(
        kern,
        grid=(batch, A_HEADS, nq),
        in_specs=[
            pl.BlockSpec((tq, A_V_DIM), lambda b, h, qi: (b * nq + qi, h)),
            pl.BlockSpec((seq, A_V_DIM), lambda b, h, qi: (b, A_HEADS + h)),
            pl.BlockSpec((seq, A_V_DIM), lambda b, h, qi: (b, 2 * A_HEADS + h)),
            vec(A_QK_DIM), vec(A_QK_DIM), vec(A_QK_DIM), vec(A_QK_DIM), vec(A_V_DIM),
        ],
        out_specs=pl.BlockSpec((tq, A_V_DIM), lambda b, h, qi: (b * nq + qi, h)),
        out_shape=jax.ShapeDtypeStruct((tokens, A_HEADS * A_V_DIM), BF16),
        scratch_shapes=[
            pltpu.VMEM((A_V_DIM, 2 * tq), BF16),
            pltpu.VMEM((seq // tk, VT_ROWS, tk), BF16),
            pltpu.VMEM((8, 2 * tq), F32),
            pltpu.VMEM((VT_ROWS, 2 * tq), F32),
            pltpu.VMEM((2, tk, 2 * tq), F32),
            pltpu.VMEM((2, 8, 2 * tq), F32),
        ],
        compiler_params=_params(3),
        name="diff_attn",
    )(z, z, z, lq1, lk1, lq2, lk2, subln_g)


def _band_window(prev_ref, cur_ref, next_ref, base, lo, hi, per, hs):
    parts = []
    if lo < 0:
        parts.append(prev_ref[base + per + lo:base + per + min(hi, 0), hs])
    if hi > 0 and lo < per:
        parts.append(cur_ref[base + max(lo, 0):base + min(hi, per), hs])
    if hi > per:
        parts.append(next_ref[base + max(lo, per) - per:base + hi - per, hs])
    return parts[0] if len(parts) == 1 else jnp.concatenate(parts, axis=0)


def _band_attn_kernel(q_ref, kp_ref, kc_ref, kn_ref, vp_ref, vc_ref, vn_ref, o_ref, lse_ref, of_ref, lf_ref,
                      *, dilation, length, chains):
    t = pl.program_id(1)
    per = TILE // dilation
    sub = min(per, BAND_SUB)
    ncls = BAND_SUB // sub
    nkeys = sub + 2 * BAND_HALF
    shape = (ncls * sub, ncls * nkeys)
    rows = lax.broadcasted_iota(jnp.int32, shape, 0)
    cols = lax.broadcasted_iota(jnp.int32, shape, 1)
    row_cls, row_pos = rows // sub, rows % sub
    col_cls, col_pos = cols // nkeys, cols % nkeys
    in_band = (jnp.abs(col_pos - BAND_HALF - row_pos) <= BAND_HALF) & (row_cls == col_cls)

    def window(refs, classes, lo, hi, hs):
        parts = [_band_window(*refs, rho * per, lo, hi, per, hs) for rho in classes]
        return parts[0] if len(parts) == 1 else jnp.concatenate(parts, axis=0)

    for i in range(per // sub):
        kpos = t * per + i * sub - BAND_HALF + col_pos
        mask = in_band & (kpos >= 0) & (kpos < length)
        lo, hi = i * sub - BAND_HALF, i * sub + sub + BAND_HALF
        tiles = [(range(rho0, rho0 + ncls), h) for rho0 in range(0, dilation, ncls)
                 for h in range(B_HEADS_PER_GROUP)]
        for b0 in range(0, len(tiles), chains):
            batch = tiles[b0:b0 + chains]
            scores = []
            for classes, h in batch:
                hs = slice(h * B_HEAD_DIM, (h + 1) * B_HEAD_DIM)
                q = q_ref[classes[0] * per + i * sub:classes[0] * per + i * sub + ncls * sub, hs]
                k = window((kp_ref, kc_ref, kn_ref), classes, lo, hi, hs)
                scores.append(lax.dot_general(q, k, (((1,), (1,)), ((), ())), preferred_element_type=F32))
            stats = []
            for s in scores:
                s = jnp.where(mask, s, NEG_BIG)
                m = jnp.max(s, axis=-1, keepdims=True)
                p = jnp.exp(s - m)
                stats.append((m, jnp.sum(p, axis=-1, keepdims=True), p.astype(BF16)))
            for (classes, h), (m, l, p) in zip(batch, stats):
                hs = slice(h * B_HEAD_DIM, (h + 1) * B_HEAD_DIM)
                v = window((vp_ref, vc_ref, vn_ref), classes, lo, hi, hs)
                o = jnp.dot(p, v, preferred_element_type=F32) / l
                lse = jnp.broadcast_to(m + jnp.log(l), (ncls * sub, B_HEAD_DIM))
                for n, rho in enumerate(classes):
                    out_rows = pl.ds(i * sub * dilation + rho, sub, stride=dilation) if dilation > 1 \
                        else pl.ds(i * sub, sub)
                    of_ref[h, out_rows, :] = o[n * sub:(n + 1) * sub]
                    lf_ref[h, out_rows, :] = lse[n * sub:(n + 1) * sub]
    for h in range(B_HEADS_PER_GROUP):
        hs = slice(h * B_HEAD_DIM, (h + 1) * B_HEAD_DIM)
        o_ref[:, hs] = of_ref[h].astype(o_ref.dtype)
        lse_ref[:, hs] = lf_ref[h]


def _band_attn(z, batch, seq, group):
    dilation = B_GROUPS[group][1]
    assert B_GROUPS[group][0] // (2 * dilation) == BAND_HALF
    nt = seq // TILE

    def tile(blk, shift):
        return pl.BlockSpec((TILE, COL_BLK),
                            lambda b, t: (b * nt + jnp.clip(t + shift, 0, nt - 1), blk + group))

    out_spec = pl.BlockSpec((TILE, B_OUT_W), lambda b, t: (b * nt + t, 0))
    kern = functools.partial(_band_attn_kernel, dilation=dilation, length=seq // dilation,
                             chains=BAND_CHAINS[group])
    return pl.pallas_call(
        kern,
        grid=(batch, nt),
        in_specs=[tile(QB_BLK, 0), tile(KB_BLK, -1), tile(KB_BLK, 0), tile(KB_BLK, 1),
                  tile(VB_BLK, -1), tile(VB_BLK, 0), tile(VB_BLK, 1)],
        out_specs=[out_spec, out_spec],
        out_shape=[jax.ShapeDtypeStruct((batch * seq, B_OUT_W), BF16),
                   jax.ShapeDtypeStruct((batch * seq, B_OUT_W), F32)],
        scratch_shapes=[pltpu.VMEM((B_HEADS_PER_GROUP, TILE, B_HEAD_DIM), F32),
                        pltpu.VMEM((B_HEADS_PER_GROUP, TILE, B_HEAD_DIM), F32)],
        compiler_params=_params(2),
        name=f"band_attn_g{group}",
    )(z, z, z, z, z, z, z)


def _merge_out_kernel(x_ref, oa_ref, o0_ref, o1_ref, o2_ref, l0_ref, l1_ref, l2_ref, *rest):
    n_c = D_MODEL // COL_BLK
    ga_refs, gb_refs = rest[:n_c], rest[n_c:2 * n_c]
    wpa_ref, wpb_ref, wout_ref, y_ref = rest[2 * n_c:]
    l0, l1, l2 = l0_ref[...], l1_ref[...], l2_ref[...]
    mx = jnp.maximum(jnp.maximum(l0, l1), l2)
    e0, e1, e2 = jnp.exp(l0 - mx), jnp.exp(l1 - mx), jnp.exp(l2 - mx)
    num = (e0 * o0_ref[...].astype(F32) + e1 * o1_ref[...].astype(F32) + e2 * o2_ref[...].astype(F32))
    ob = (num / (e0 + e1 + e2)).astype(BF16)
    ya = jnp.dot(oa_ref[...], wpa_ref[...], preferred_element_type=F32)
    yb = jnp.dot(ob, wpb_ref[...], preferred_element_type=F32)
    merged = []
    for c in range(n_c):
        cs = slice(c * COL_BLK, (c + 1) * COL_BLK)
        merged.append((jax.nn.sigmoid(ga_refs[c][...].astype(F32)) * ya[:, cs]
                       + jax.nn.sigmoid(gb_refs[c][...].astype(F32)) * yb[:, cs]).astype(BF16))
    merged = jnp.concatenate(merged, axis=1)
    y_ref[...] = x_ref[...] + jnp.dot(merged, wout_ref[...], preferred_element_type=F32)


def _merge_out(x2d, oa, ob_groups, z, wpa, wpb, wout, tm):
    tokens = x2d.shape[0]
    row = lambda w: pl.BlockSpec((tm, w), lambda i: (i, 0))
    gate = lambda blk: pl.BlockSpec((tm, COL_BLK), lambda i: (i, blk))
    whole = lambda a: pl.BlockSpec(a.shape, lambda i: (0, 0), pipeline_mode=pl.Buffered(1))
    n_c = D_MODEL // COL_BLK
    (o0, l0), (o1, l1), (o2, l2) = ob_groups
    return pl.pallas_call(
        _merge_out_kernel,
        grid=(tokens // tm,),
        in_specs=[row(D_MODEL), row(A_HEADS * A_V_DIM)] + [row(B_OUT_W)] * 6
        + [gate(GA_BLK + c) for c in range(n_c)] + [gate(GB_BLK + c) for c in range(n_c)]
        + [whole(wpa), whole(wpb), whole(wout)],
        out_specs=row(D_MODEL),
        out_shape=jax.ShapeDtypeStruct((tokens, D_MODEL), F32),
        compiler_params=_params(1),
        name="merge_out",
    )(x2d, oa, o0, o1, o2, l0, l1, l2, *([z] * (2 * n_c)), wpa, wpb, wout)


def _ffn_kernel(x_ref, g_ref, w1_ref, w2_ref, gf_ref, y_ref, h_ref, acc_ref, *, final_norm):
    f = pl.program_id(1)

    @pl.when(f == 0)
    def _():
        x = x_ref[...]
        ms = jnp.mean(x * x, axis=-1, keepdims=True)
        h_ref[...] = (x * lax.rsqrt(ms + NORM_EPS) * g_ref[...]).astype(h_ref.dtype)
        acc_ref[...] = x

    u = jnp.dot(h_ref[...], w1_ref[...], preferred_element_type=F32)
    u = jnp.square(jnp.maximum(u, 0.0)).astype(BF16)
    acc_ref[...] += jnp.dot(u, w2_ref[...], preferred_element_type=F32)

    @pl.when(f == pl.num_programs(1) - 1)
    def _():
        y = acc_ref[...]
        if final_norm:
            ms = jnp.mean(y * y, axis=-1, keepdims=True)
            y = y * lax.rsqrt(ms + NORM_EPS) * gf_ref[...]
        y_ref[...] = y


def _ffn(x2d, g, w1, w2, g_final, final_norm, tm, tf):
    tokens = x2d.shape[0]
    kern = functools.partial(_ffn_kernel, final_norm=final_norm)
    return pl.pallas_call(
        kern,
        grid=(tokens // tm, D_FF // tf),
        in_specs=[
            pl.BlockSpec((tm, D_MODEL), lambda i, f: (i, 0)),
            pl.BlockSpec((1, D_MODEL), lambda i, f: (0, 0)),
            pl.BlockSpec((D_MODEL, tf), lambda i, f: (0, f)),
            pl.BlockSpec((tf, D_MODEL), lambda i, f: (f, 0)),
            pl.BlockSpec((1, D_MODEL), lambda i, f: (0, 0)),
        ],
        out_specs=pl.BlockSpec((tm, D_MODEL), lambda i, f: (i, 0)),
        out_shape=jax.ShapeDtypeStruct((tokens, D_MODEL), F32),
        scratch_shapes=[pltpu.VMEM((tm, D_MODEL), BF16), pltpu.VMEM((tm, D_MODEL), F32)],
        compiler_params=_params(2),
        name="ffn",
    )(x2d, g, w1, w2, g_final)


def _position_tables(seq):
    cos_a, sin_a = _rope_tables(seq, A_QK_DIM, (1,))
    return (cos_a[0], sin_a[0]), _rope_tables(seq, B_HEAD_DIM, [dilation for _, dilation in B_GROUPS])


def _trunk(x, tabs, norm_mix, w_in, lambda_q1, lambda_k1, lambda_q2, lambda_k2, subln_g,
           w_proj_a, w_proj_b, w_out, norm_ffn, w1, w2, norm_final):
    batch, seq, _ = x.shape
    assert seq % TILE == 0
    depth = w_in.shape[0]
    x2d = x.reshape(batch * seq, D_MODEL)
    tabs_a, tabs_b = tabs
    row = lambda v: v.reshape(1, -1)
    tq, tk = (1024, 512) if seq <= 4096 else (512, 1024)
    for l in range(depth):
        lambda_init = 0.8 - 0.6 * math.exp(-0.3 * l)
        z = _in_proj(x2d, seq, row(norm_mix[l]), w_in[l], tabs_a, tabs_b)
        oa = _diff_attn(z, batch, seq, row(lambda_q1[l]), row(lambda_k1[l]), row(lambda_q2[l]),
                        row(lambda_k2[l]), row(subln_g[l]), lambda_init, tq=tq, tk=tk)
        ob_groups = [_band_attn(z, batch, seq, g) for g in range(N_GROUPS)]
        x2d = _merge_out(x2d, oa, ob_groups, z, w_proj_a[l], w_proj_b[l], w_out[l], tm=256)
        x2d = _ffn(x2d, row(norm_ffn[l]), w1[l], w2[l], row(norm_final),
                   final_norm=(l == depth - 1), tm=512, tf=1024)
    return x2d.reshape(batch, seq, D_MODEL)


def kernel(x_prompt, x_sample, norm_mix, w_in, lambda_q1, lambda_k1, lambda_q2, lambda_k2, subln_g,
           w_proj_a, w_proj_b, w_out, norm_ffn, w1, w2, norm_final):
    w_in_blocks = _permute_rotary_columns(w_in.astype(BF16)).reshape(
        w_in.shape[0], D_MODEL, N_COL_BLKS, COL_BLK).transpose(0, 2, 1, 3)
    weights = (norm_mix, w_in_blocks, lambda_q1, lambda_k1, lambda_q2,
               lambda_k2, subln_g, w_proj_a.astype(BF16), w_proj_b.astype(BF16), w_out.astype(BF16), norm_ffn,
               w1.astype(BF16), w2.astype(BF16), norm_final)
    tabs = _position_tables(max(x_prompt.shape[1], x_sample.shape[1]))
    return _trunk(x_prompt, tabs, *weights), _trunk(x_sample, tabs, *weights)
```

```python
import functools
import math

import jax
import jax.numpy as jnp
import numpy as np
from jax import lax
from jax.experimental import pallas as pl
from jax.experimental.pallas import tpu as pltpu

D_MODEL = 2048
A_HEADS = 8
A_QK_DIM = 64
A_V_DIM = 2 * A_QK_DIM
B_GROUPS = ((128, 1), (512, 4), (2048, 16))
B_HEADS_PER_GROUP = 4
B_HEAD_DIM = 128
B_OUT_W = B_HEADS_PER_GROUP * B_HEAD_DIM
D_FF = 4 * D_MODEL
ROPE_THETA = 500000.0
ROPE_FRACTION_DEN = 4
NORM_EPS = 1e-6
SUBLN_EPS = 1e-5
NEG_BIG = -1e30
LOG2E = 1.4426950408889634

COL_BLK = 512
IN_WIDTH = 11776
N_COL_BLKS = IN_WIDTH // COL_BLK
QA_BLK, KA_BLK, VA_BLK = 0, 2, 4
QB_BLK, KB_BLK, VB_BLK = 6, 9, 12
GA_BLK, GB_BLK = 15, 19
N_GROUPS = len(B_GROUPS)
LANES = 128
TILE = 1024
BAND_HALF = 64
BAND_SUB = 128
KEY_PART = 256
BAND_CHAINS = (1, 1, 16)
VT_ROWS = A_V_DIM + 16

VMEM_LIMIT = 56 * 1024 * 1024

BF16 = jnp.bfloat16
F32 = jnp.float32


def _params(n_axes):
    return pltpu.CompilerParams(dimension_semantics=("arbitrary",) * n_axes,
                                vmem_limit_bytes=VMEM_LIMIT)


def _rotary_layout(head_dim):
    half = head_dim // ROPE_FRACTION_DEN // 2
    heads = LANES // head_dim
    first = [h * head_dim + i for h in range(heads) for i in range(half)]
    second = [h * head_dim + half + i for h in range(heads) for i in range(half)]
    target1 = list(range(len(first)))
    target2 = list(range(LANES // 2, LANES // 2 + len(second)))
    src = np.arange(LANES)
    freq = np.full(LANES, -1)
    sign = np.zeros(LANES)
    for lanes, cols, sgn in ((target1, first, -1.0), (target2, second, 1.0)):
        for j, (lane, col) in enumerate(zip(lanes, cols)):
            src[lane], freq[lane], sign[lane] = col, j % half, sgn
    rotary, targets = set(first + second), set(target1 + target2)
    for lane, col in zip(sorted(rotary - targets), sorted(targets - rotary)):
        src[lane] = col
    assert sorted(src) == list(range(LANES))
    return src, freq, sign


def _runs(idx):
    out, start = [], 0
    for i in range(1, len(idx) + 1):
        if i == len(idx) or idx[i] != idx[i - 1] + 1:
            out.append((int(idx[start]), int(idx[i - 1]) + 1))
            start = i
    return out


def _block_layout(j):
    return jnp.where(j < VA_BLK, 1, jnp.where((j >= QB_BLK) & (j < VB_BLK), 2, 0))


def _prep_w_in_kernel(w_ref, perm_ref, o_ref):
    o_ref[...] = jnp.dot(w_ref[0].astype(BF16), perm_ref[...], preferred_element_type=F32).astype(o_ref.dtype)


def _prep_w_in(w_in):
    depth = w_in.shape[0]
    perms = []
    for head_dim in (None, A_QK_DIM, B_HEAD_DIM):
        src = np.arange(LANES) if head_dim is None else _rotary_layout(head_dim)[0]
        src = np.concatenate([src + g * LANES for g in range(COL_BLK // LANES)])
        perms.append(jnp.arange(COL_BLK)[:, None] == jnp.asarray(src)[None, :])
    perms = jnp.stack(perms).astype(BF16)
    return pl.pallas_call(
        _prep_w_in_kernel,
        grid=(depth, N_COL_BLKS),
        in_specs=[pl.BlockSpec((1, D_MODEL, COL_BLK), lambda l, j: (l, 0, j)),
                  pl.BlockSpec((None, COL_BLK, COL_BLK), lambda l, j: (_block_layout(j), 0, 0))],
        out_specs=pl.BlockSpec((None, None, D_MODEL, COL_BLK), lambda l, j: (l, j, 0, 0)),
        out_shape=jax.ShapeDtypeStruct((depth, N_COL_BLKS, D_MODEL, COL_BLK), BF16),
        compiler_params=_params(2),
        name="prep_w_in",
    )(w_in, perms)


def _rope_tables(seq, head_dim, dilations):
    rot = head_dim // ROPE_FRACTION_DEN
    inv = ROPE_THETA ** (-jnp.arange(0, rot, 2, dtype=F32) / rot)
    _, freq, sign = _rotary_layout(head_dim)
    inv_lane = jnp.where(jnp.asarray(freq >= 0), inv[np.maximum(freq, 0)], 0.0)[None, :]
    sign = jnp.asarray(sign, F32)[None, :]
    base = jnp.arange(0, seq, TILE, dtype=F32)[:, None] * inv_lane
    cos_a, sin_a = jnp.cos(base)[:, None, :], jnp.sin(base)[:, None, :]
    cos_tabs, sin_tabs = [], []
    for dilation in dilations:
        offset = np.arange(TILE).reshape(TILE // dilation, dilation).T.reshape(TILE)
        ang = jnp.asarray(offset, F32)[:, None] * inv_lane
        cos_b, sin_b = jnp.cos(ang)[None], jnp.sin(ang)[None]
        cos_tabs.append((cos_a * cos_b - sin_a * sin_b).reshape(seq, LANES))
        sin_tabs.append(((sin_a * cos_b + cos_a * sin_b) * sign).reshape(seq, LANES))
    return jnp.stack(cos_tabs), jnp.stack(sin_tabs)


def _rope_store(acc, cos, sin, scale, z_ref):
    for c in range(COL_BLK // LANES):
        xc = acc[:, c * LANES:(c + 1) * LANES]
        out = xc * cos + pltpu.roll(xc, LANES // 2, 1) * sin
        if scale != 1.0:
            out = out * scale
        z_ref[:, c * LANES:(c + 1) * LANES] = out.astype(z_ref.dtype)


def _row_order(j):
    return jnp.where((j >= QB_BLK) & (j < GA_BLK), (j - QB_BLK) % N_GROUPS, 0)


def _in_proj_kernel(x_ref, g_ref, w_ref, ca_ref, sa_ref, cb_ref, sb_ref, z_ref, h_ref, slab_ref):
    j = pl.program_id(1)

    @pl.when(j == 0)
    def _():
        x = x_ref[...]
        rinv = lax.rsqrt(jnp.mean(x * x, axis=-1, keepdims=True) + NORM_EPS)
        for c in range(D_MODEL // LANES):
            cs = slice(c * LANES, (c + 1) * LANES)
            slab = x_ref[:, cs] * rinv * g_ref[:, cs]
            h_ref[0, :, cs] = slab.astype(h_ref.dtype)
            slab_ref[...] = slab
            for order in range(1, N_GROUPS):
                dilation = B_GROUPS[order][1]
                per = TILE // dilation
                for rho in range(dilation):
                    h_ref[order, rho * per:(rho + 1) * per, cs] = (
                        slab_ref[pl.ds(rho, per, stride=dilation), :].astype(h_ref.dtype))

    def project():
        return jnp.dot(h_ref[_row_order(j)], w_ref[...], preferred_element_type=F32)

    @pl.when(j < KA_BLK)
    def _():
        _rope_store(project(), ca_ref[...], sa_ref[...], A_QK_DIM ** -0.5 * LOG2E, z_ref)

    @pl.when((j >= KA_BLK) & (j < VA_BLK))
    def _():
        _rope_store(project(), ca_ref[...], sa_ref[...], 1.0, z_ref)

    @pl.when((j >= QB_BLK) & (j < KB_BLK))
    def _():
        _rope_store(project(), cb_ref[...], sb_ref[...], B_HEAD_DIM ** -0.5, z_ref)

    @pl.when((j >= KB_BLK) & (j < VB_BLK))
    def _():
        _rope_store(project(), cb_ref[...], sb_ref[...], 1.0, z_ref)

    @pl.when(((j >= VA_BLK) & (j < QB_BLK)) | (j >= VB_BLK))
    def _():
        z_ref[...] = project().astype(z_ref.dtype)


def _in_proj(x2d, seq, g, w_bf16, tabs_a, tabs_b):
    tokens = x2d.shape[0]
    nt = seq // TILE
    tab_a = pl.BlockSpec((TILE, LANES), lambda i, j: (i % nt, 0))
    tab_b = pl.BlockSpec((None, TILE, LANES), lambda i, j: (_row_order(j), i % nt, 0))
    return pl.pallas_call(
        _in_proj_kernel,
        grid=(tokens // TILE, N_COL_BLKS),
        in_specs=[
            pl.BlockSpec((TILE, D_MODEL), lambda i, j: (i, 0)),
            pl.BlockSpec((1, D_MODEL), lambda i, j: (0, 0)),
            pl.BlockSpec((None, D_MODEL, COL_BLK), lambda i, j: (j, 0, 0)),
            tab_a, tab_a, tab_b, tab_b,
        ],
        out_specs=pl.BlockSpec((TILE, COL_BLK), lambda i, j: (i, j)),
        out_shape=jax.ShapeDtypeStruct((tokens, IN_WIDTH), BF16),
        scratch_shapes=[pltpu.VMEM((N_GROUPS, TILE, D_MODEL), BF16), pltpu.VMEM((TILE, LANES), F32)],
        compiler_params=_params(2),
        name="in_proj",
    )(x2d, g, w_bf16, *tabs_a, *tabs_b)


_MAP0_LANES = _runs(np.flatnonzero(_rotary_layout(A_QK_DIM)[0] < A_QK_DIM))


def _diff_attn_kernel(q_ref, k_ref, v_ref, lq1_ref, lk1_ref, lq2_ref, lk2_ref, g_ref, o_ref,
                      q2t_ref, vt_ref, m_ref, acc_ref, s_ref, cmax_ref, *, tq, tk, seq, lambda_init):
    qi = pl.program_id(2)
    nck = seq // tk

    @pl.when(qi == 0)
    def _():
        def transpose_chunk(c, carry):
            rows = pl.ds(pl.multiple_of(c * tk, tk), tk)
            vt_ref[c, :A_V_DIM, :] = v_ref[rows, :].astype(F32).T.astype(BF16)
            sub = lax.broadcasted_iota(jnp.int32, (VT_ROWS - A_V_DIM, tk), 0)
            vt_ref[c, A_V_DIM:, :] = jnp.where(sub == 0, 1.0, 0.0).astype(BF16)
            return carry
        lax.fori_loop(0, nck, transpose_chunk, 0)

    qt = q_ref[...].astype(F32).T
    dim = lax.broadcasted_iota(jnp.int32, (A_V_DIM, 1), 0)
    map0 = functools.reduce(jnp.logical_or, [(dim >= a) & (dim < b) for a, b in _MAP0_LANES])
    q2t_ref[:, :tq] = jnp.where(map0, qt, 0.0).astype(BF16)
    q2t_ref[:, tq:] = jnp.where(map0, 0.0, qt).astype(BF16)
    m_ref[...] = jnp.full(m_ref.shape, NEG_BIG, F32)
    acc_ref[...] = jnp.zeros(acc_ref.shape, F32)

    nparts = tk // KEY_PART

    def scores_part(c, slot, j):
        rows = pl.ds(pl.multiple_of(c * tk, tk) + j * KEY_PART, KEY_PART)
        s = jnp.dot(k_ref[rows, :], q2t_ref[...], preferred_element_type=F32)
        s_ref[slot, j * KEY_PART:(j + 1) * KEY_PART, :] = s
        return jnp.max(s, axis=0, keepdims=True)

    def scores(c, slot, do):
        cmax = None
        for j in range(nparts):
            part_max = scores_part(c, slot, j)
            cmax = part_max if cmax is None else jnp.maximum(cmax, part_max)
            do(j)
        cmax_ref[slot] = jnp.broadcast_to(cmax, cmax_ref.shape[1:])

    def step(c, slot, with_next):
        m_old = m_ref[0:1, :]
        m_new = jnp.maximum(m_old, cmax_ref[slot, 0:1, :])
        acc = [jnp.exp2(m_old - m_new) * acc_ref[...]]

        def pv_part(j):
            keys = slice(j * KEY_PART, (j + 1) * KEY_PART)
            p = jnp.exp2(s_ref[slot, keys, :] - m_new).astype(BF16)
            acc[0] = acc[0] + jnp.dot(vt_ref[c, :, keys], p, preferred_element_type=F32)

        if with_next:
            scores(c + 1, 1 - slot, pv_part)
        else:
            for j in range(nparts):
                pv_part(j)
        acc_ref[...] = acc[0]
        m_ref[...] = jnp.broadcast_to(m_new, m_ref.shape)

    scores(0, 0, lambda j: None)
    group = 4 if nck % 4 == 0 else 2

    def trip(c0, last):
        for u in range(group):
            step(c0 + u, u % 2, not (last and u == group - 1))

    def body(i, carry):
        trip(group * i, False)
        return carry

    lax.fori_loop(0, nck // group - 1, body, 0)
    trip(nck - group, True)

    acc = acc_ref[...]
    ot = acc[:A_V_DIM] / acc[A_V_DIM:A_V_DIM + 1]
    lam = (jnp.exp(jnp.sum(lq1_ref[...] * lk1_ref[...], axis=-1, keepdims=True))
           - jnp.exp(jnp.sum(lq2_ref[...] * lk2_ref[...], axis=-1, keepdims=True)) + lambda_init)
    o = ot[:, :tq] - lam * ot[:, tq:]
    ms = jnp.mean(o * o, axis=0, keepdims=True)
    o = o * lax.rsqrt(ms + SUBLN_EPS)
    o_ref[...] = (o.T * g_ref[...] * (1.0 - lambda_init)).astype(o_ref.dtype)


def _diff_attn(z, batch, seq, lq1, lk1, lq2, lk2, subln_g, lambda_init, tq, tk):
    tokens = batch * seq
    nq = seq // tq
    assert seq % (2 * tk) == 0
    vec = lambda n: pl.BlockSpec((1, n), lambda b, h, qi: (0, 0))
    kern = functools.partial(_diff_attn_kernel, tq=tq, tk=tk, seq=seq, lambda_init=lambda_init)
    return pl.pallas_call(
        kern,
        grid=(batch, A_HEADS, nq),
        in_specs=[
            pl.BlockSpec((tq, A_V_DIM), lambda b, h, qi: (b * nq + qi, h)),
            pl.BlockSpec((seq, A_V_DIM), lambda b, h, qi: (b, A_HEADS + h)),
            pl.BlockSpec((seq, A_V_DIM), lambda b, h, qi: (b, 2 * A_HEADS + h)),
            vec(A_QK_DIM), vec(A_QK_DIM), vec(A_QK_DIM), vec(A_QK_DIM), vec(A_V_DIM),
        ],
        out_specs=pl.BlockSpec((tq, A_V_DIM), lambda b, h, qi: (b * nq + qi, h)),
        out_shape=jax.ShapeDtypeStruct((tokens, A_HEADS * A_V_DIM), BF16),
        scratch_shapes=[
            pltpu.VMEM((A_V_DIM, 2 * tq), BF16),
            pltpu.VMEM((seq // tk, VT_ROWS, tk), BF16),
            pltpu.VMEM((8, 2 * tq), F32),
            pltpu.VMEM((VT_ROWS, 2 * tq), F32),
            pltpu.VMEM((2, tk, 2 * tq), F32),
            pltpu.VMEM((2, 8, 2 * tq), F32),
        ],
        compiler_params=_params(3),
        name="diff_attn",
    )(z, z, z, lq1, lk1, lq2, lk2, subln_g)


def _band_window(prev_ref, cur_ref, next_ref, base, lo, hi, per, hs):
    parts = []
    if lo < 0:
        parts.append(prev_ref[base + per + lo:base + per + min(hi, 0), hs])
    if hi > 0 and lo < per:
        parts.append(cur_ref[base + max(lo, 0):base + min(hi, per), hs])
    if hi > per:
        parts.append(next_ref[base + max(lo, per) - per:base + hi - per, hs])
    return parts[0] if len(parts) == 1 else jnp.concatenate(parts, axis=0)


def _band_attn_kernel(q_ref, kp_ref, kc_ref, kn_ref, vp_ref, vc_ref, vn_ref, o_ref, lse_ref, of_ref, lf_ref,
                      *, dilation, length, chains):
    t = pl.program_id(1)
    per = TILE // dilation
    sub = min(per, BAND_SUB)
    ncls = BAND_SUB // sub
    nkeys = sub + 2 * BAND_HALF
    shape = (ncls * sub, ncls * nkeys)
    rows = lax.broadcasted_iota(jnp.int32, shape, 0)
    cols = lax.broadcasted_iota(jnp.int32, shape, 1)
    row_cls, row_pos = rows // sub, rows % sub
    col_cls, col_pos = cols // nkeys, cols % nkeys
    in_band = (jnp.abs(col_pos - BAND_HALF - row_pos) <= BAND_HALF) & (row_cls == col_cls)

    def window(refs, classes, lo, hi, hs):
        parts = [_band_window(*refs, rho * per, lo, hi, per, hs) for rho in classes]
        return parts[0] if len(parts) == 1 else jnp.concatenate(parts, axis=0)

    for i in range(per // sub):
        kpos = t * per + i * sub - BAND_HALF + col_pos
        mask = in_band & (kpos >= 0) & (kpos < length)
        lo, hi = i * sub - BAND_HALF, i * sub + sub + BAND_HALF
        tiles = [(range(rho0, rho0 + ncls), h) for rho0 in range(0, dilation, ncls)
                 for h in range(B_HEADS_PER_GROUP)]
        for b0 in range(0, len(tiles), chains):
            batch = tiles[b0:b0 + chains]
            scores = []
            for classes, h in batch:
                hs = slice(h * B_HEAD_DIM, (h + 1) * B_HEAD_DIM)
                q = q_ref[classes[0] * per + i * sub:classes[0] * per + i * sub + ncls * sub, hs]
                k = window((kp_ref, kc_ref, kn_ref), classes, lo, hi, hs)
                scores.append(lax.dot_general(q, k, (((1,), (1,)), ((), ())), preferred_element_type=F32))
            stats = []
            for s in scores:
                s = jnp.where(mask, s, NEG_BIG)
                m = jnp.max(s, axis=-1, keepdims=True)
                p = jnp.exp(s - m)
                stats.append((m, jnp.sum(p, axis=-1, keepdims=True), p.astype(BF16)))
            for (classes, h), (m, l, p) in zip(batch, stats):
                hs = slice(h * B_HEAD_DIM, (h + 1) * B_HEAD_DIM)
                v = window((vp_ref, vc_ref, vn_ref), classes, lo, hi, hs)
                o = jnp.dot(p, v, preferred_element_type=F32) / l
                lse = jnp.broadcast_to(m + jnp.log(l), (ncls * sub, B_HEAD_DIM))
                for n, rho in enumerate(classes):
                    out_rows = pl.ds(i * sub * dilation + rho, sub, stride=dilation) if dilation > 1 \
                        else pl.ds(i * sub, sub)
                    of_ref[h, out_rows, :] = o[n * sub:(n + 1) * sub]
                    lf_ref[h, out_rows, :] = lse[n * sub:(n + 1) * sub]
    for h in range(B_HEADS_PER_GROUP):
        hs = slice(h * B_HEAD_DIM, (h + 1) * B_HEAD_DIM)
        o_ref[:, hs] = of_ref[h].astype(o_ref.dtype)
        lse_ref[:, hs] = lf_ref[h]


def _band_attn(z, batch, seq, group):
    dilation = B_GROUPS[group][1]
    assert B_GROUPS[group][0] // (2 * dilation) == BAND_HALF
    nt = seq // TILE

    def tile(blk, shift):
        return pl.BlockSpec((TILE, COL_BLK),
                            lambda b, t: (b * nt + jnp.clip(t + shift, 0, nt - 1), blk + group))

    out_spec = pl.BlockSpec((TILE, B_OUT_W), lambda b, t: (b * nt + t, 0))
    kern = functools.partial(_band_attn_kernel, dilation=dilation, length=seq // dilation,
                             chains=BAND_CHAINS[group])
    return pl.pallas_call(
        kern,
        grid=(batch, nt),
        in_specs=[tile(QB_BLK, 0), tile(KB_BLK, -1), tile(KB_BLK, 0), tile(KB_BLK, 1),
                  tile(VB_BLK, -1), tile(VB_BLK, 0), tile(VB_BLK, 1)],
        out_specs=[out_spec, out_spec],
        out_shape=[jax.ShapeDtypeStruct((batch * seq, B_OUT_W), BF16),
                   jax.ShapeDtypeStruct((batch * seq, B_OUT_W), F32)],
        scratch_shapes=[pltpu.VMEM((B_HEADS_PER_GROUP, TILE, B_HEAD_DIM), F32),
                        pltpu.VMEM((B_HEADS_PER_GROUP, TILE, B_HEAD_DIM), F32)],
        compiler_params=_params(2),
        name=f"band_attn_g{group}",
    )(z, z, z, z, z, z, z)


def _merge_out_kernel(x_ref, oa_ref, o0_ref, o1_ref, o2_ref, l0_ref, l1_ref, l2_ref, *rest):
    n_c = D_MODEL // COL_BLK
    ga_refs, gb_refs = rest[:n_c], rest[n_c:2 * n_c]
    wpa_ref, wpb_ref, wout_ref, y_ref = rest[2 * n_c:]
    l0, l1, l2 = l0_ref[...], l1_ref[...], l2_ref[...]
    mx = jnp.maximum(jnp.maximum(l0, l1), l2)
    e0, e1, e2 = jnp.exp(l0 - mx), jnp.exp(l1 - mx), jnp.exp(l2 - mx)
    num = (e0 * o0_ref[...].astype(F32) + e1 * o1_ref[...].astype(F32) + e2 * o2_ref[...].astype(F32))
    ob = (num / (e0 + e1 + e2)).astype(BF16)
    ya = jnp.dot(oa_ref[...], wpa_ref[...], preferred_element_type=F32)
    yb = jnp.dot(ob, wpb_ref[...], preferred_element_type=F32)
    merged = []
    for c in range(n_c):
        cs = slice(c * COL_BLK, (c + 1) * COL_BLK)
        merged.append((jax.nn.sigmoid(ga_refs[c][...].astype(F32)) * ya[:, cs]
                       + jax.nn.sigmoid(gb_refs[c][...].astype(F32)) * yb[:, cs]).astype(BF16))
    merged = jnp.concatenate(merged, axis=1)
    y_ref[...] = x_ref[...] + jnp.dot(merged, wout_ref[...], preferred_element_type=F32)


def _merge_out(x2d, oa, ob_groups, z, wpa, wpb, wout, tm):
    tokens = x2d.shape[0]
    row = lambda w: pl.BlockSpec((tm, w), lambda i: (i, 0))
    gate = lambda blk: pl.BlockSpec((tm, COL_BLK), lambda i: (i, blk))
    whole = lambda a: pl.BlockSpec(a.shape, lambda i: (0, 0), pipeline_mode=pl.Buffered(1))
    n_c = D_MODEL // COL_BLK
    (o0, l0), (o1, l1), (o2, l2) = ob_groups
    return pl.pallas_call(
        _merge_out_kernel,
        grid=(tokens // tm,),
        in_specs=[row(D_MODEL), row(A_HEADS * A_V_DIM)] + [row(B_OUT_W)] * 6
        + [gate(GA_BLK + c) for c in range(n_c)] + [gate(GB_BLK + c) for c in range(n_c)]
        + [whole(wpa), whole(wpb), whole(wout)],
        out_specs=row(D_MODEL),
        out_shape=jax.ShapeDtypeStruct((tokens, D_MODEL), F32),
        compiler_params=_params(1),
        name="merge_out",
    )(x2d, oa, o0, o1, o2, l0, l1, l2, *([z] * (2 * n_c)), wpa, wpb, wout)


def _ffn_kernel(x_ref, g_ref, w1_ref, w2_ref, gf_ref, y_ref, h_ref, acc_ref, *, final_norm):
    f = pl.program_id(1)

    @pl.when(f == 0)
    def _():
        x = x_ref[...]
        ms = jnp.mean(x * x, axis=-1, keepdims=True)
        h_ref[...] = (x * lax.rsqrt(ms + NORM_EPS) * g_ref[...]).astype(h_ref.dtype)
        acc_ref[...] = x

    u = jnp.dot(h_ref[...], w1_ref[...], preferred_element_type=F32)
    u = jnp.square(jnp.maximum(u, 0.0)).astype(BF16)
    acc_ref[...] += jnp.dot(u, w2_ref[...], preferred_element_type=F32)

    @pl.when(f == pl.num_programs(1) - 1)
    def _():
        y = acc_ref[...]
        if final_norm:
            ms = jnp.mean(y * y, axis=-1, keepdims=True)
            y = y * lax.rsqrt(ms + NORM_EPS) * gf_ref[...]
        y_ref[...] = y


def _ffn(x2d, g, w1, w2, g_final, final_norm, tm, tf):
    tokens = x2d.shape[0]
    kern = functools.partial(_ffn_kernel, final_norm=final_norm)
    return pl.pallas_call(
        kern,
        grid=(tokens // tm, D_FF // tf),
        in_specs=[
            pl.BlockSpec((tm, D_MODEL), lambda i, f: (i, 0)),
            pl.BlockSpec((1, D_MODEL), lambda i, f: (0, 0)),
            pl.BlockSpec((D_MODEL, tf), lambda i, f: (0, f)),
            pl.BlockSpec((tf, D_MODEL), lambda i, f: (f, 0)),
            pl.BlockSpec((1, D_MODEL), lambda i, f: (0, 0)),
        ],
        out_specs=pl.BlockSpec((tm, D_MODEL), lambda i, f: (i, 0)),
        out_shape=jax.ShapeDtypeStruct((tokens, D_MODEL), F32),
        scratch_shapes=[pltpu.VMEM((tm, D_MODEL), BF16), pltpu.VMEM((tm, D_MODEL), F32)],
        compiler_params=_params(2),
        name="ffn",
    )(x2d, g, w1, w2, g_final)


def _position_tables(seq):
    cos_a, sin_a = _rope_tables(seq, A_QK_DIM, (1,))
    return (cos_a[0], sin_a[0]), _rope_tables(seq, B_HEAD_DIM, [dilation for _, dilation in B_GROUPS])


def _trunk(x, tabs, norm_mix, w_in, lambda_q1, lambda_k1, lambda_q2, lambda_k2, subln_g,
           w_proj_a, w_proj_b, w_out, norm_ffn, w1, w2, norm_final):
    batch, seq, _ = x.shape
    assert seq % TILE == 0
    depth = w_in.shape[0]
    x2d = x.reshape(batch * seq, D_MODEL)
    tabs_a, tabs_b = tabs
    row = lambda v: v.reshape(1, -1)
    tq, tk = (1024, 512) if seq <= 4096 else (512, 1024)
    for l in range(depth):
        lambda_init = 0.8 - 0.6 * math.exp(-0.3 * l)
        z = _in_proj(x2d, seq, row(norm_mix[l]), w_in[l], tabs_a, tabs_b)
        oa = _diff_attn(z, batch, seq, row(lambda_q1[l]), row(lambda_k1[l]), row(lambda_q2[l]),
                        row(lambda_k2[l]), row(subln_g[l]), lambda_init, tq=tq, tk=tk)
        ob_groups = [_band_attn(z, batch, seq, g) for g in range(N_GROUPS)]
        x2d = _merge_out(x2d, oa, ob_groups, z, w_proj_a[l], w_proj_b[l], w_out[l], tm=256)
        x2d = _ffn(x2d, row(norm_ffn[l]), w1[l], w2[l], row(norm_final),
                   final_norm=(l == depth - 1), tm=512, tf=1024)
    return x2d.reshape(batch, seq, D_MODEL)


def kernel(x_prompt, x_sample, norm_mix, w_in, lambda_q1, lambda_k1, lambda_q2, lambda_k2, subln_g,
           w_proj_a, w_proj_b, w_out, norm_ffn, w1, w2, norm_final):
    weights = (norm_mix, _prep_w_in(w_in), lambda_q1, lambda_k1, lambda_q2,
               lambda_k2, subln_g, w_proj_a.astype(BF16), w_proj_b.astype(BF16), w_out.astype(BF16), norm_ffn,
               w1.astype(BF16), w2.astype(BF16), norm_final)
    tabs = _position_tables(max(x_prompt.shape[1], x_sample.shape[1]))
    return _trunk(x_prompt, tabs, *weights), _trunk(x_sample, tabs, *weights)
```

```python
import functools
import math

import jax
import jax.numpy as jnp
import numpy as np
from jax import lax
from jax.experimental import pallas as pl
from jax.experimental.pallas import tpu as pltpu

D_MODEL = 2048
A_HEADS = 8
A_QK_DIM = 64
A_V_DIM = 2 * A_QK_DIM
B_GROUPS = ((128, 1), (512, 4), (2048, 16))
B_HEADS_PER_GROUP = 4
B_HEAD_DIM = 128
B_OUT_W = B_HEADS_PER_GROUP * B_HEAD_DIM
D_FF = 4 * D_MODEL
ROPE_THETA = 500000.0
ROPE_FRACTION_DEN = 4
NORM_EPS = 1e-6
SUBLN_EPS = 1e-5
NEG_BIG = -1e30
LOG2E = 1.4426950408889634

COL_BLK = 512
IN_WIDTH = 11776
N_COL_BLKS = IN_WIDTH // COL_BLK
QA_BLK, KA_BLK, VA_BLK = 0, 2, 4
QB_BLK, KB_BLK, VB_BLK = 6, 9, 12
GA_BLK, GB_BLK = 15, 19
N_GROUPS = len(B_GROUPS)
LANES = 128
TILE = 1024
ROW_PARTS = 4
BAND_HALF = 64
BAND_SUB = 128
KEY_PART = 256
BAND_CHAINS = (1, 1, 16)
VT_ROWS = A_V_DIM + 16

VMEM_LIMIT = 56 * 1024 * 1024

BF16 = jnp.bfloat16
F32 = jnp.float32


def _params(n_axes):
    return pltpu.CompilerParams(dimension_semantics=("arbitrary",) * n_axes,
                                vmem_limit_bytes=VMEM_LIMIT)


def _rotary_layout(head_dim):
    half = head_dim // ROPE_FRACTION_DEN // 2
    heads = LANES // head_dim
    first = [h * head_dim + i for h in range(heads) for i in range(half)]
    second = [h * head_dim + half + i for h in range(heads) for i in range(half)]
    target1 = list(range(len(first)))
    target2 = list(range(LANES // 2, LANES // 2 + len(second)))
    src = np.arange(LANES)
    freq = np.full(LANES, -1)
    sign = np.zeros(LANES)
    for lanes, cols, sgn in ((target1, first, -1.0), (target2, second, 1.0)):
        for j, (lane, col) in enumerate(zip(lanes, cols)):
            src[lane], freq[lane], sign[lane] = col, j % half, sgn
    rotary, targets = set(first + second), set(target1 + target2)
    for lane, col in zip(sorted(rotary - targets), sorted(targets - rotary)):
        src[lane] = col
    assert sorted(src) == list(range(LANES))
    return src, freq, sign


def _runs(idx):
    out, start = [], 0
    for i in range(1, len(idx) + 1):
        if i == len(idx) or idx[i] != idx[i - 1] + 1:
            out.append((int(idx[start]), int(idx[i - 1]) + 1))
            start = i
    return out


def _block_layout(j):
    return jnp.where(j < VA_BLK, 1, jnp.where((j >= QB_BLK) & (j < VB_BLK), 2, 0))


def _prep_w_in_kernel(w_ref, perm_ref, o_ref):
    o_ref[...] = jnp.dot(w_ref[0].astype(BF16), perm_ref[...], preferred_element_type=F32).astype(o_ref.dtype)


def _prep_w_in(w_in):
    depth = w_in.shape[0]
    perms = []
    for head_dim in (None, A_QK_DIM, B_HEAD_DIM):
        src = np.arange(LANES) if head_dim is None else _rotary_layout(head_dim)[0]
        src = np.concatenate([src + g * LANES for g in range(COL_BLK // LANES)])
        perms.append(jnp.arange(COL_BLK)[:, None] == jnp.asarray(src)[None, :])
    perms = jnp.stack(perms).astype(BF16)
    return pl.pallas_call(
        _prep_w_in_kernel,
        grid=(depth, N_COL_BLKS),
        in_specs=[pl.BlockSpec((1, D_MODEL, COL_BLK), lambda l, j: (l, 0, j)),
                  pl.BlockSpec((None, COL_BLK, COL_BLK), lambda l, j: (_block_layout(j), 0, 0))],
        out_specs=pl.BlockSpec((None, None, D_MODEL, COL_BLK), lambda l, j: (l, j, 0, 0)),
        out_shape=jax.ShapeDtypeStruct((depth, N_COL_BLKS, D_MODEL, COL_BLK), BF16),
        compiler_params=_params(2),
        name="prep_w_in",
    )(w_in, perms)


def _rope_tables(seq, head_dim, dilations):
    rot = head_dim // ROPE_FRACTION_DEN
    inv = ROPE_THETA ** (-jnp.arange(0, rot, 2, dtype=F32) / rot)
    _, freq, sign = _rotary_layout(head_dim)
    inv_lane = jnp.where(jnp.asarray(freq >= 0), inv[np.maximum(freq, 0)], 0.0)[None, :]
    sign = jnp.asarray(sign, F32)[None, :]
    base = jnp.arange(0, seq, TILE, dtype=F32)[:, None] * inv_lane
    cos_a, sin_a = jnp.cos(base)[:, None, :], jnp.sin(base)[:, None, :]
    cos_tabs, sin_tabs = [], []
    for dilation in dilations:
        offset = np.arange(TILE).reshape(TILE // dilation, dilation).T.reshape(TILE)
        ang = jnp.asarray(offset, F32)[:, None] * inv_lane
        cos_b, sin_b = jnp.cos(ang)[None], jnp.sin(ang)[None]
        cos_tabs.append((cos_a * cos_b - sin_a * sin_b).reshape(seq, LANES))
        sin_tabs.append(((sin_a * cos_b + cos_a * sin_b) * sign).reshape(seq, LANES))
    return jnp.stack(cos_tabs), jnp.stack(sin_tabs)


def _rope_store(project, cos_ref, sin_ref, scale, z_ref):
    for r in range(ROW_PARTS):
        rows = slice(r * (TILE // ROW_PARTS), (r + 1) * (TILE // ROW_PARTS))
        acc, cos, sin = project(rows), cos_ref[rows, :], sin_ref[rows, :]
        for c in range(COL_BLK // LANES):
            xc = acc[:, c * LANES:(c + 1) * LANES]
            out = xc * cos + pltpu.roll(xc, LANES // 2, 1) * sin
            if scale != 1.0:
                out = out * scale
            z_ref[rows, c * LANES:(c + 1) * LANES] = out.astype(z_ref.dtype)


def _row_order(j):
    return jnp.where((j >= QB_BLK) & (j < GA_BLK), (j - QB_BLK) % N_GROUPS, 0)


def _in_proj_kernel(x_ref, g_ref, w_ref, ca_ref, sa_ref, cb_ref, sb_ref, z_ref, h_ref, slab_ref):
    j = pl.program_id(1)

    @pl.when(j == 0)
    def _():
        x = x_ref[...]
        rinv = lax.rsqrt(jnp.mean(x * x, axis=-1, keepdims=True) + NORM_EPS)
        for c in range(D_MODEL // LANES):
            cs = slice(c * LANES, (c + 1) * LANES)
            slab = x_ref[:, cs] * rinv * g_ref[:, cs]
            h_ref[0, :, cs] = slab.astype(h_ref.dtype)
            slab_ref[...] = slab
            for order in range(1, N_GROUPS):
                dilation = B_GROUPS[order][1]
                per = TILE // dilation
                for rho in range(dilation):
                    h_ref[order, rho * per:(rho + 1) * per, cs] = (
                        slab_ref[pl.ds(rho, per, stride=dilation), :].astype(h_ref.dtype))

    def project(rows):
        return jnp.dot(h_ref[_row_order(j), rows, :], w_ref[...], preferred_element_type=F32)

    @pl.when(j < KA_BLK)
    def _():
        _rope_store(project, ca_ref, sa_ref, A_QK_DIM ** -0.5 * LOG2E, z_ref)

    @pl.when((j >= KA_BLK) & (j < VA_BLK))
    def _():
        _rope_store(project, ca_ref, sa_ref, 1.0, z_ref)

    @pl.when((j >= QB_BLK) & (j < KB_BLK))
    def _():
        _rope_store(project, cb_ref, sb_ref, B_HEAD_DIM ** -0.5, z_ref)

    @pl.when((j >= KB_BLK) & (j < VB_BLK))
    def _():
        _rope_store(project, cb_ref, sb_ref, 1.0, z_ref)

    @pl.when(((j >= VA_BLK) & (j < QB_BLK)) | (j >= VB_BLK))
    def _():
        for r in range(ROW_PARTS):
            rows = slice(r * (TILE // ROW_PARTS), (r + 1) * (TILE // ROW_PARTS))
            z_ref[rows, :] = project(rows).astype(z_ref.dtype)


def _in_proj(x2d, seq, g, w_bf16, tabs_a, tabs_b):
    tokens = x2d.shape[0]
    nt = seq // TILE
    tab_a = pl.BlockSpec((TILE, LANES), lambda i, j: (i % nt, 0))
    tab_b = pl.BlockSpec((None, TILE, LANES), lambda i, j: (_row_order(j), i % nt, 0))
    return pl.pallas_call(
        _in_proj_kernel,
        grid=(tokens // TILE, N_COL_BLKS),
        in_specs=[
            pl.BlockSpec((TILE, D_MODEL), lambda i, j: (i, 0)),
            pl.BlockSpec((1, D_MODEL), lambda i, j: (0, 0)),
            pl.BlockSpec((None, D_MODEL, COL_BLK), lambda i, j: (j, 0, 0)),
            tab_a, tab_a, tab_b, tab_b,
        ],
        out_specs=pl.BlockSpec((TILE, COL_BLK), lambda i, j: (i, j)),
        out_shape=jax.ShapeDtypeStruct((tokens, IN_WIDTH), BF16),
        scratch_shapes=[pltpu.VMEM((N_GROUPS, TILE, D_MODEL), BF16), pltpu.VMEM((TILE, LANES), F32)],
        compiler_params=_params(2),
        name="in_proj",
    )(x2d, g, w_bf16, *tabs_a, *tabs_b)


_MAP0_LANES = _runs(np.flatnonzero(_rotary_layout(A_QK_DIM)[0] < A_QK_DIM))


def _diff_attn_kernel(q_ref, k_ref, v_ref, lq1_ref, lk1_ref, lq2_ref, lk2_ref, g_ref, o_ref,
                      q2t_ref, vt_ref, m_ref, acc_ref, s_ref, cmax_ref, *, tq, tk, seq, lambda_init):
    qi = pl.program_id(2)
    nck = seq // tk

    @pl.when(qi == 0)
    def _():
        def transpose_chunk(c, carry):
            rows = pl.ds(pl.multiple_of(c * tk, tk), tk)
            vt_ref[c, :A_V_DIM, :] = v_ref[rows, :].astype(F32).T.astype(BF16)
            sub = lax.broadcasted_iota(jnp.int32, (VT_ROWS - A_V_DIM, tk), 0)
            vt_ref[c, A_V_DIM:, :] = jnp.where(sub == 0, 1.0, 0.0).astype(BF16)
            return carry
        lax.fori_loop(0, nck, transpose_chunk, 0)

    qt = q_ref[...].astype(F32).T
    dim = lax.broadcasted_iota(jnp.int32, (A_V_DIM, 1), 0)
    map0 = functools.reduce(jnp.logical_or, [(dim >= a) & (dim < b) for a, b in _MAP0_LANES])
    q2t_ref[:, :tq] = jnp.where(map0, qt, 0.0).astype(BF16)
    q2t_ref[:, tq:] = jnp.where(map0, 0.0, qt).astype(BF16)
    m_ref[...] = jnp.full(m_ref.shape, NEG_BIG, F32)
    acc_ref[...] = jnp.zeros(acc_ref.shape, F32)

    nparts = tk // KEY_PART

    def scores_part(c, slot, j):
        rows = pl.ds(pl.multiple_of(c * tk, tk) + j * KEY_PART, KEY_PART)
        s = jnp.dot(k_ref[rows, :], q2t_ref[...], preferred_element_type=F32)
        s_ref[slot, j * KEY_PART:(j + 1) * KEY_PART, :] = s
        return jnp.max(s, axis=0, keepdims=True)

    def scores(c, slot, do):
        cmax = None
        for j in range(nparts):
            part_max = scores_part(c, slot, j)
            cmax = part_max if cmax is None else jnp.maximum(cmax, part_max)
            do(j)
        cmax_ref[slot] = jnp.broadcast_to(cmax, cmax_ref.shape[1:])

    def step(c, slot, with_next):
        m_old = m_ref[0:1, :]
        m_new = jnp.maximum(m_old, cmax_ref[slot, 0:1, :])
        acc = [jnp.exp2(m_old - m_new) * acc_ref[...]]

        def pv_part(j):
            keys = slice(j * KEY_PART, (j + 1) * KEY_PART)
            p = jnp.exp2(s_ref[slot, keys, :] - m_new).astype(BF16)
            acc[0] = acc[0] + jnp.dot(vt_ref[c, :, keys], p, preferred_element_type=F32)

        if with_next:
            scores(c + 1, 1 - slot, pv_part)
        else:
            for j in range(nparts):
                pv_part(j)
        acc_ref[...] = acc[0]
        m_ref[...] = jnp.broadcast_to(m_new, m_ref.shape)

    scores(0, 0, lambda j: None)
    group = 4 if nck % 4 == 0 else 2

    def trip(c0, last):
        for u in range(group):
            step(c0 + u, u % 2, not (last and u == group - 1))

    def body(i, carry):
        trip(group * i, False)
        return carry

    lax.fori_loop(0, nck // group - 1, body, 0)
    trip(nck - group, True)

    acc = acc_ref[...]
    ot = acc[:A_V_DIM] / acc[A_V_DIM:A_V_DIM + 1]
    lam = (jnp.exp(jnp.sum(lq1_ref[...] * lk1_ref[...], axis=-1, keepdims=True))
           - jnp.exp(jnp.sum(lq2_ref[...] * lk2_ref[...], axis=-1, keepdims=True)) + lambda_init)
    o = ot[:, :tq] - lam * ot[:, tq:]
    ms = jnp.mean(o * o, axis=0, keepdims=True)
    o = o * lax.rsqrt(ms + SUBLN_EPS)
    o_ref[...] = (o.T * g_ref[...] * (1.0 - lambda_init)).astype(o_ref.dtype)


def _diff_attn(z, batch, seq, lq1, lk1, lq2, lk2, subln_g, lambda_init, tq, tk):
    tokens = batch * seq
    nq = seq // tq
    assert seq % (2 * tk) == 0
    vec = lambda n: pl.BlockSpec((1, n), lambda b, h, qi: (0, 0))
    kern = functools.partial(_diff_attn_kernel, tq=tq, tk=tk, seq=seq, lambda_init=lambda_init)
    return pl.pallas_call(
        kern,
        grid=(batch, A_HEADS, nq),
        in_specs=[
            pl.BlockSpec((tq, A_V_DIM), lambda b, h, qi: (b * nq + qi, h)),
            pl.BlockSpec((seq, A_V_DIM), lambda b, h, qi: (b, A_HEADS + h)),
            pl.BlockSpec((seq, A_V_DIM), lambda b, h, qi: (b, 2 * A_HEADS + h)),
            vec(A_QK_DIM), vec(A_QK_DIM), vec(A_QK_DIM), vec(A_QK_DIM), vec(A_V_DIM),
        ],
        out_specs=pl.BlockSpec((tq, A_V_DIM), lambda b, h, qi: (b * nq + qi, h)),
        out_shape=jax.ShapeDtypeStruct((tokens, A_HEADS * A_V_DIM), BF16),
        scratch_shapes=[
            pltpu.VMEM((A_V_DIM, 2 * tq), BF16),
            pltpu.VMEM((seq // tk, VT_ROWS, tk), BF16),
            pltpu.VMEM((8, 2 * tq), F32),
            pltpu.VMEM((VT_ROWS, 2 * tq), F32),
            pltpu.VMEM((2, tk, 2 * tq), F32),
            pltpu.VMEM((2, 8, 2 * tq), F32),
        ],
        compiler_params=_params(3),
        name="diff_attn",
    )(z, z, z, lq1, lk1, lq2, lk2, subln_g)


def _band_window(prev_ref, cur_ref, next_ref, base, lo, hi, per, hs):
    parts = []
    if lo < 0:
        parts.append(prev_ref[base + per + lo:base + per + min(hi, 0), hs])
    if hi > 0 and lo < per:
        parts.append(cur_ref[base + max(lo, 0):base + min(hi, per), hs])
    if hi > per:
        parts.append(next_ref[base + max(lo, per) - per:base + hi - per, hs])
    return parts[0] if len(parts) == 1 else jnp.concatenate(parts, axis=0)


def _band_attn_kernel(q_ref, kp_ref, kc_ref, kn_ref, vp_ref, vc_ref, vn_ref, o_ref, lse_ref, of_ref, lf_ref,
                      *, dilation, length, chains):
    t = pl.program_id(1)
    per = TILE // dilation
    sub = min(per, BAND_SUB)
    ncls = BAND_SUB // sub
    nkeys = sub + 2 * BAND_HALF
    shape = (ncls * sub, ncls * nkeys)
    rows = lax.broadcasted_iota(jnp.int32, shape, 0)
    cols = lax.broadcasted_iota(jnp.int32, shape, 1)
    row_cls, row_pos = rows // sub, rows % sub
    col_cls, col_pos = cols // nkeys, cols % nkeys
    in_band = (jnp.abs(col_pos - BAND_HALF - row_pos) <= BAND_HALF) & (row_cls == col_cls)

    def window(refs, classes, lo, hi, hs):
        parts = [_band_window(*refs, rho * per, lo, hi, per, hs) for rho in classes]
        return parts[0] if len(parts) == 1 else jnp.concatenate(parts, axis=0)

    for i in range(per // sub):
        kpos = t * per + i * sub - BAND_HALF + col_pos
        mask = in_band & (kpos >= 0) & (kpos < length)
        lo, hi = i * sub - BAND_HALF, i * sub + sub + BAND_HALF
        tiles = [(range(rho0, rho0 + ncls), h) for rho0 in range(0, dilation, ncls)
                 for h in range(B_HEADS_PER_GROUP)]
        for b0 in range(0, len(tiles), chains):
            batch = tiles[b0:b0 + chains]
            scores = []
            for classes, h in batch:
                hs = slice(h * B_HEAD_DIM, (h + 1) * B_HEAD_DIM)
                q = q_ref[classes[0] * per + i * sub:classes[0] * per + i * sub + ncls * sub, hs]
                k = window((kp_ref, kc_ref, kn_ref), classes, lo, hi, hs)
                scores.append(lax.dot_general(q, k, (((1,), (1,)), ((), ())), preferred_element_type=F32))
            stats = []
            for s in scores:
                s = jnp.where(mask, s, NEG_BIG)
                m = jnp.max(s, axis=-1, keepdims=True)
                p = jnp.exp(s - m)
                stats.append((m, jnp.sum(p, axis=-1, keepdims=True), p.astype(BF16)))
            for (classes, h), (m, l, p) in zip(batch, stats):
                hs = slice(h * B_HEAD_DIM, (h + 1) * B_HEAD_DIM)
                v = window((vp_ref, vc_ref, vn_ref), classes, lo, hi, hs)
                o = jnp.dot(p, v, preferred_element_type=F32) / l
                lse = jnp.broadcast_to(m + jnp.log(l), (ncls * sub, B_HEAD_DIM))
                for n, rho in enumerate(classes):
                    out_rows = pl.ds(i * sub * dilation + rho, sub, stride=dilation) if dilation > 1 \
                        else pl.ds(i * sub, sub)
                    of_ref[h, out_rows, :] = o[n * sub:(n + 1) * sub]
                    lf_ref[h, out_rows, :] = lse[n * sub:(n + 1) * sub]
    for h in range(B_HEADS_PER_GROUP):
        hs = slice(h * B_HEAD_DIM, (h + 1) * B_HEAD_DIM)
        o_ref[:, hs] = of_ref[h].astype(o_ref.dtype)
        lse_ref[:, hs] = lf_ref[h]


def _band_attn(z, batch, seq, group):
    dilation = B_GROUPS[group][1]
    assert B_GROUPS[group][0] // (2 * dilation) == BAND_HALF
    nt = seq // TILE

    def tile(blk, shift):
        return pl.BlockSpec((TILE, COL_BLK),
                            lambda b, t: (b * nt + jnp.clip(t + shift, 0, nt - 1), blk + group))

    out_spec = pl.BlockSpec((TILE, B_OUT_W), lambda b, t: (b * nt + t, 0))
    kern = functools.partial(_band_attn_kernel, dilation=dilation, length=seq // dilation,
                             chains=BAND_CHAINS[group])
    return pl.pallas_call(
        kern,
        grid=(batch, nt),
        in_specs=[tile(QB_BLK, 0), tile(KB_BLK, -1), tile(KB_BLK, 0), tile(KB_BLK, 1),
                  tile(VB_BLK, -1), tile(VB_BLK, 0), tile(VB_BLK, 1)],
        out_specs=[out_spec, out_spec],
        out_shape=[jax.ShapeDtypeStruct((batch * seq, B_OUT_W), BF16),
                   jax.ShapeDtypeStruct((batch * seq, B_OUT_W), F32)],
        scratch_shapes=[pltpu.VMEM((B_HEADS_PER_GROUP, TILE, B_HEAD_DIM), F32),
                        pltpu.VMEM((B_HEADS_PER_GROUP, TILE, B_HEAD_DIM), F32)],
        compiler_params=_params(2),
        name=f"band_attn_g{group}",
    )(z, z, z, z, z, z, z)


def _merge_out_kernel(x_ref, oa_ref, o0_ref, o1_ref, o2_ref, l0_ref, l1_ref, l2_ref, *rest):
    n_c = D_MODEL // COL_BLK
    ga_refs, gb_refs = rest[:n_c], rest[n_c:2 * n_c]
    wpa_ref, wpb_ref, wout_ref, y_ref = rest[2 * n_c:]
    l0, l1, l2 = l0_ref[...], l1_ref[...], l2_ref[...]
    mx = jnp.maximum(jnp.maximum(l0, l1), l2)
    e0, e1, e2 = jnp.exp(l0 - mx), jnp.exp(l1 - mx), jnp.exp(l2 - mx)
    num = (e0 * o0_ref[...].astype(F32) + e1 * o1_ref[...].astype(F32) + e2 * o2_ref[...].astype(F32))
    ob = (num / (e0 + e1 + e2)).astype(BF16)
    ya = jnp.dot(oa_ref[...], wpa_ref[...], preferred_element_type=F32)
    yb = jnp.dot(ob, wpb_ref[...], preferred_element_type=F32)
    merged = []
    for c in range(n_c):
        cs = slice(c * COL_BLK, (c + 1) * COL_BLK)
        merged.append((jax.nn.sigmoid(ga_refs[c][...].astype(F32)) * ya[:, cs]
                       + jax.nn.sigmoid(gb_refs[c][...].astype(F32)) * yb[:, cs]).astype(BF16))
    merged = jnp.concatenate(merged, axis=1)
    y_ref[...] = x_ref[...] + jnp.dot(merged, wout_ref[...], preferred_element_type=F32)


def _merge_out(x2d, oa, ob_groups, z, wpa, wpb, wout, tm):
    tokens = x2d.shape[0]
    row = lambda w: pl.BlockSpec((tm, w), lambda i: (i, 0))
    gate = lambda blk: pl.BlockSpec((tm, COL_BLK), lambda i: (i, blk))
    whole = lambda a: pl.BlockSpec(a.shape, lambda i: (0, 0), pipeline_mode=pl.Buffered(1))
    n_c = D_MODEL // COL_BLK
    (o0, l0), (o1, l1), (o2, l2) = ob_groups
    return pl.pallas_call(
        _merge_out_kernel,
        grid=(tokens // tm,),
        in_specs=[row(D_MODEL), row(A_HEADS * A_V_DIM)] + [row(B_OUT_W)] * 6
        + [gate(GA_BLK + c) for c in range(n_c)] + [gate(GB_BLK + c) for c in range(n_c)]
        + [whole(wpa), whole(wpb), whole(wout)],
        out_specs=row(D_MODEL),
        out_shape=jax.ShapeDtypeStruct((tokens, D_MODEL), F32),
        compiler_params=_params(1),
        name="merge_out",
    )(x2d, oa, o0, o1, o2, l0, l1, l2, *([z] * (2 * n_c)), wpa, wpb, wout)


def _ffn_kernel(x_ref, g_ref, w1_ref, w2_ref, gf_ref, y_ref, h_ref, acc_ref, *, final_norm):
    f = pl.program_id(1)

    @pl.when(f == 0)
    def _():
        x = x_ref[...]
        ms = jnp.mean(x * x, axis=-1, keepdims=True)
        h_ref[...] = (x * lax.rsqrt(ms + NORM_EPS) * g_ref[...]).astype(h_ref.dtype)
        acc_ref[...] = x

    u = jnp.dot(h_ref[...], w1_ref[...], preferred_element_type=F32)
    u = jnp.square(jnp.maximum(u, 0.0)).astype(BF16)
    acc_ref[...] += jnp.dot(u, w2_ref[...], preferred_element_type=F32)

    @pl.when(f == pl.num_programs(1) - 1)
    def _():
        y = acc_ref[...]
        if final_norm:
            ms = jnp.mean(y * y, axis=-1, keepdims=True)
            y = y * lax.rsqrt(ms + NORM_EPS) * gf_ref[...]
        y_ref[...] = y


def _ffn(x2d, g, w1, w2, g_final, final_norm, tm, tf):
    tokens = x2d.shape[0]
    kern = functools.partial(_ffn_kernel, final_norm=final_norm)
    return pl.pallas_call(
        kern,
        grid=(tokens // tm, D_FF // tf),
        in_specs=[
            pl.BlockSpec((tm, D_MODEL), lambda i, f: (i, 0)),
            pl.BlockSpec((1, D_MODEL), lambda i, f: (0, 0)),
            pl.BlockSpec((D_MODEL, tf), lambda i, f: (0, f)),
            pl.BlockSpec((tf, D_MODEL), lambda i, f: (f, 0)),
            pl.BlockSpec((1, D_MODEL), lambda i, f: (0, 0)),
        ],
        out_specs=pl.BlockSpec((tm, D_MODEL), lambda i, f: (i, 0)),
        out_shape=jax.ShapeDtypeStruct((tokens, D_MODEL), F32),
        scratch_shapes=[pltpu.VMEM((tm, D_MODEL), BF16), pltpu.VMEM((tm, D_MODEL), F32)],
        compiler_params=_params(2),
        name="ffn",
    )(x2d, g, w1, w2, g_final)


def _position_tables(seq):
    cos_a, sin_a = _rope_tables(seq, A_QK_DIM, (1,))
    return (cos_a[0], sin_a[0]), _rope_tables(seq, B_HEAD_DIM, [dilation for _, dilation in B_GROUPS])


def _trunk(x, tabs, norm_mix, w_in, lambda_q1, lambda_k1, lambda_q2, lambda_k2, subln_g,
           w_proj_a, w_proj_b, w_out, norm_ffn, w1, w2, norm_final):
    batch, seq, _ = x.shape
    assert seq % TILE == 0
    depth = w_in.shape[0]
    x2d = x.reshape(batch * seq, D_MODEL)
    tabs_a, tabs_b = tabs
    row = lambda v: v.reshape(1, -1)
    tq, tk = (1024, 512) if seq <= 4096 else (512, 1024)
    for l in range(depth):
        lambda_init = 0.8 - 0.6 * math.exp(-0.3 * l)
        z = _in_proj(x2d, seq, row(norm_mix[l]), w_in[l], tabs_a, tabs_b)
        oa = _diff_attn(z, batch, seq, row(lambda_q1[l]), row(lambda_k1[l]), row(lambda_q2[l]),
                        row(lambda_k2[l]), row(subln_g[l]), lambda_init, tq=tq, tk=tk)
        ob_groups = [_band_attn(z, batch, seq, g) for g in range(N_GROUPS)]
        x2d = _merge_out(x2d, oa, ob_groups, z, w_proj_a[l], w_proj_b[l], w_out[l], tm=256)
        x2d = _ffn(x2d, row(norm_ffn[l]), w1[l], w2[l], row(norm_final),
                   final_norm=(l == depth - 1), tm=512, tf=1024)
    return x2d.reshape(batch, seq, D_MODEL)


def kernel(x_prompt, x_sample, norm_mix, w_in, lambda_q1, lambda_k1, lambda_q2, lambda_k2, subln_g,
           w_proj_a, w_proj_b, w_out, norm_ffn, w1, w2, norm_final):
    weights = (norm_mix, _prep_w_in(w_in), lambda_q1, lambda_k1, lambda_q2,
               lambda_k2, subln_g, w_proj_a.astype(BF16), w_proj_b.astype(BF16), w_out.astype(BF16), norm_ffn,
               w1.astype(BF16), w2.astype(BF16), norm_final)
    tabs = _position_tables(max(x_prompt.shape[1], x_sample.shape[1]))
    return _trunk(x_prompt, tabs, *weights), _trunk(x_sample, tabs, *weights)
```

```python
import functools
import math

import jax
import jax.numpy as jnp
import numpy as np
from jax import lax
from jax.experimental import pallas as pl
from jax.experimental.pallas import tpu as pltpu

D_MODEL = 2048
A_HEADS = 8
A_QK_DIM = 64
A_V_DIM = 2 * A_QK_DIM
B_GROUPS = ((128, 1), (512, 4), (2048, 16))
B_HEADS_PER_GROUP = 4
B_HEAD_DIM = 128
B_OUT_W = B_HEADS_PER_GROUP * B_HEAD_DIM
D_FF = 4 * D_MODEL
ROPE_THETA = 500000.0
ROPE_FRACTION_DEN = 4
NORM_EPS = 1e-6
SUBLN_EPS = 1e-5
NEG_BIG = -1e30
LOG2E = 1.4426950408889634

COL_BLK = 512
IN_WIDTH = 11776
N_COL_BLKS = IN_WIDTH // COL_BLK
QA_BLK, KA_BLK, VA_BLK = 0, 2, 4
QB_BLK, KB_BLK, VB_BLK = 6, 9, 12
GA_BLK, GB_BLK = 15, 19
N_GROUPS = len(B_GROUPS)
LANES = 128
TILE = 1024
ROW_PARTS = 4
BAND_HALF = 64
BAND_SUB = 128
KEY_PART = 256
BAND_CHAINS = (1, 1, 16)
VT_ROWS = A_V_DIM + 16

VMEM_LIMIT = 56 * 1024 * 1024

BF16 = jnp.bfloat16
F32 = jnp.float32


def _params(n_axes):
    return pltpu.CompilerParams(dimension_semantics=("arbitrary",) * n_axes,
                                vmem_limit_bytes=VMEM_LIMIT)


def _rotary_layout(head_dim):
    half = head_dim // ROPE_FRACTION_DEN // 2
    heads = LANES // head_dim
    first = [h * head_dim + i for h in range(heads) for i in range(half)]
    second = [h * head_dim + half + i for h in range(heads) for i in range(half)]
    target1 = list(range(len(first)))
    target2 = list(range(LANES // 2, LANES // 2 + len(second)))
    src = np.arange(LANES)
    freq = np.full(LANES, -1)
    sign = np.zeros(LANES)
    for lanes, cols, sgn in ((target1, first, -1.0), (target2, second, 1.0)):
        for j, (lane, col) in enumerate(zip(lanes, cols)):
            src[lane], freq[lane], sign[lane] = col, j % half, sgn
    rotary, targets = set(first + second), set(target1 + target2)
    for lane, col in zip(sorted(rotary - targets), sorted(targets - rotary)):
        src[lane] = col
    assert sorted(src) == list(range(LANES))
    return src, freq, sign


def _runs(idx):
    out, start = [], 0
    for i in range(1, len(idx) + 1):
        if i == len(idx) or idx[i] != idx[i - 1] + 1:
            out.append((int(idx[start]), int(idx[i - 1]) + 1))
            start = i
    return out


def _block_layout(j):
    return jnp.where(j < VA_BLK, 1, jnp.where((j >= QB_BLK) & (j < VB_BLK), 2, 0))


def _prep_w_in_kernel(w_ref, perm_ref, o_ref):
    o_ref[...] = jnp.dot(w_ref[0].astype(BF16), perm_ref[...], preferred_element_type=F32).astype(o_ref.dtype)


def _prep_w_in(w_in):
    depth = w_in.shape[0]
    perms = []
    for head_dim in (None, A_QK_DIM, B_HEAD_DIM):
        src = np.arange(LANES) if head_dim is None else _rotary_layout(head_dim)[0]
        src = np.concatenate([src + g * LANES for g in range(COL_BLK // LANES)])
        perms.append(jnp.arange(COL_BLK)[:, None] == jnp.asarray(src)[None, :])
    perms = jnp.stack(perms).astype(BF16)
    return pl.pallas_call(
        _prep_w_in_kernel,
        grid=(depth, N_COL_BLKS),
        in_specs=[pl.BlockSpec((1, D_MODEL, COL_BLK), lambda l, j: (l, 0, j)),
                  pl.BlockSpec((None, COL_BLK, COL_BLK), lambda l, j: (_block_layout(j), 0, 0))],
        out_specs=pl.BlockSpec((None, None, D_MODEL, COL_BLK), lambda l, j: (l, j, 0, 0)),
        out_shape=jax.ShapeDtypeStruct((depth, N_COL_BLKS, D_MODEL, COL_BLK), BF16),
        compiler_params=_params(2),
        name="prep_w_in",
    )(w_in, perms)


def _rope_tables(seq, head_dim, dilations):
    rot = head_dim // ROPE_FRACTION_DEN
    inv = ROPE_THETA ** (-jnp.arange(0, rot, 2, dtype=F32) / rot)
    _, freq, sign = _rotary_layout(head_dim)
    inv_lane = jnp.where(jnp.asarray(freq >= 0), inv[np.maximum(freq, 0)], 0.0)[None, :]
    sign = jnp.asarray(sign, F32)[None, :]
    base = jnp.arange(0, seq, TILE, dtype=F32)[:, None] * inv_lane
    cos_a, sin_a = jnp.cos(base)[:, None, :], jnp.sin(base)[:, None, :]
    cos_tabs, sin_tabs = [], []
    for dilation in dilations:
        offset = np.arange(TILE).reshape(TILE // dilation, dilation).T.reshape(TILE)
        ang = jnp.asarray(offset, F32)[:, None] * inv_lane
        cos_b, sin_b = jnp.cos(ang)[None], jnp.sin(ang)[None]
        cos_tabs.append((cos_a * cos_b - sin_a * sin_b).reshape(seq, LANES))
        sin_tabs.append(((sin_a * cos_b + cos_a * sin_b) * sign).reshape(seq, LANES))
    return jnp.stack(cos_tabs), jnp.stack(sin_tabs)


def _rope_store(project, cos_ref, sin_ref, scale, z_ref):
    for r in range(ROW_PARTS):
        rows = slice(r * (TILE // ROW_PARTS), (r + 1) * (TILE // ROW_PARTS))
        acc, cos, sin = project(rows), cos_ref[rows, :], sin_ref[rows, :]
        for c in range(COL_BLK // LANES):
            xc = acc[:, c * LANES:(c + 1) * LANES]
            out = xc * cos + pltpu.roll(xc, LANES // 2, 1) * sin
            if scale != 1.0:
                out = out * scale
            z_ref[rows, c * LANES:(c + 1) * LANES] = out.astype(z_ref.dtype)


def _row_order(j):
    return jnp.where((j >= QB_BLK) & (j < GA_BLK), (j - QB_BLK) % N_GROUPS, 0)


def _in_proj_kernel(x_ref, g_ref, w_ref, ca_ref, sa_ref, cb_ref, sb_ref, z_ref, h_ref, slab_ref):
    j = pl.program_id(1)

    @pl.when(j == 0)
    def _():
        x = x_ref[...]
        rinv = lax.rsqrt(jnp.mean(x * x, axis=-1, keepdims=True) + NORM_EPS)
        for c in range(D_MODEL // LANES):
            cs = slice(c * LANES, (c + 1) * LANES)
            slab = x_ref[:, cs] * rinv * g_ref[:, cs]
            h_ref[0, :, cs] = slab.astype(h_ref.dtype)
            slab_ref[...] = slab
            for order in range(1, N_GROUPS):
                dilation = B_GROUPS[order][1]
                per = TILE // dilation
                for rho in range(dilation):
                    h_ref[order, rho * per:(rho + 1) * per, cs] = (
                        slab_ref[pl.ds(rho, per, stride=dilation), :].astype(h_ref.dtype))

    def project(rows):
        return jnp.dot(h_ref[_row_order(j), rows, :], w_ref[...], preferred_element_type=F32)

    @pl.when(j < KA_BLK)
    def _():
        _rope_store(project, ca_ref, sa_ref, A_QK_DIM ** -0.5 * LOG2E, z_ref)

    @pl.when((j >= KA_BLK) & (j < VA_BLK))
    def _():
        _rope_store(project, ca_ref, sa_ref, 1.0, z_ref)

    @pl.when((j >= QB_BLK) & (j < KB_BLK))
    def _():
        _rope_store(project, cb_ref, sb_ref, B_HEAD_DIM ** -0.5, z_ref)

    @pl.when((j >= KB_BLK) & (j < VB_BLK))
    def _():
        _rope_store(project, cb_ref, sb_ref, 1.0, z_ref)

    @pl.when(((j >= VA_BLK) & (j < QB_BLK)) | (j >= VB_BLK))
    def _():
        for r in range(ROW_PARTS):
            rows = slice(r * (TILE // ROW_PARTS), (r + 1) * (TILE // ROW_PARTS))
            z_ref[rows, :] = project(rows).astype(z_ref.dtype)


def _in_proj(x2d, seq, g, w_bf16, tabs_a, tabs_b):
    tokens = x2d.shape[0]
    nt = seq // TILE
    tab_a = pl.BlockSpec((TILE, LANES), lambda i, j: (i % nt, 0))
    tab_b = pl.BlockSpec((None, TILE, LANES), lambda i, j: (_row_order(j), i % nt, 0))
    return pl.pallas_call(
        _in_proj_kernel,
        grid=(tokens // TILE, N_COL_BLKS),
        in_specs=[
            pl.BlockSpec((TILE, D_MODEL), lambda i, j: (i, 0)),
            pl.BlockSpec((1, D_MODEL), lambda i, j: (0, 0)),
            pl.BlockSpec((None, D_MODEL, COL_BLK), lambda i, j: (j, 0, 0)),
            tab_a, tab_a, tab_b, tab_b,
        ],
        out_specs=pl.BlockSpec((TILE, COL_BLK), lambda i, j: (i, j)),
        out_shape=jax.ShapeDtypeStruct((tokens, IN_WIDTH), BF16),
        scratch_shapes=[pltpu.VMEM((N_GROUPS, TILE, D_MODEL), BF16), pltpu.VMEM((TILE, LANES), F32)],
        compiler_params=_params(2),
        name="in_proj",
    )(x2d, g, w_bf16, *tabs_a, *tabs_b)


_MAP0_LANES = _runs(np.flatnonzero(_rotary_layout(A_QK_DIM)[0] < A_QK_DIM))


def _diff_attn_kernel(q_ref, k_ref, v_ref, lq1_ref, lk1_ref, lq2_ref, lk2_ref, g_ref, o_ref,
                      q2t_ref, vt_ref, m_ref, acc_ref, s_ref, cmax_ref, *, tq, tk, seq, nsub, lambda_init):
    qi = pl.program_id(2)
    nck = seq // tk

    @pl.when(qi == 0)
    def _():
        def transpose_chunk(c, carry):
            rows = pl.ds(pl.multiple_of(c * tk, tk), tk)
            vt_ref[c, :A_V_DIM, :] = v_ref[rows, :].astype(F32).T.astype(BF16)
            sub = lax.broadcasted_iota(jnp.int32, (VT_ROWS - A_V_DIM, tk), 0)
            vt_ref[c, A_V_DIM:, :] = jnp.where(sub == 0, 1.0, 0.0).astype(BF16)
            return carry
        lax.fori_loop(0, nck, transpose_chunk, 0)

    dim = lax.broadcasted_iota(jnp.int32, (A_V_DIM, 1), 0)
    map0 = functools.reduce(jnp.logical_or, [(dim >= a) & (dim < b) for a, b in _MAP0_LANES])
    for sub in range(nsub):
        qt = q_ref[sub * tq:(sub + 1) * tq, :].astype(F32).T
        q2t_ref[sub, :, :tq] = jnp.where(map0, qt, 0.0).astype(BF16)
        q2t_ref[sub, :, tq:] = jnp.where(map0, 0.0, qt).astype(BF16)

    nparts = tk // KEY_PART

    def scores_part(sub, c, slot, j):
        rows = pl.ds(pl.multiple_of(c * tk, tk) + j * KEY_PART, KEY_PART)
        s = jnp.dot(k_ref[rows, :], q2t_ref[sub], preferred_element_type=F32)
        s_ref[slot, j * KEY_PART:(j + 1) * KEY_PART, :] = s
        return jnp.max(s, axis=0, keepdims=True)

    def scores(sub, c, slot, do):
        cmax = None
        for j in range(nparts):
            part_max = scores_part(sub, c, slot, j)
            cmax = part_max if cmax is None else jnp.maximum(cmax, part_max)
            do(j)
        cmax_ref[slot] = jnp.broadcast_to(cmax, cmax_ref.shape[1:])

    def step(sub, c, slot, nxt):
        m_old = m_ref[0:1, :]
        m_new = jnp.maximum(m_old, cmax_ref[slot, 0:1, :])
        acc = [jnp.exp2(m_old - m_new) * acc_ref[...]]

        def pv_part(j):
            keys = slice(j * KEY_PART, (j + 1) * KEY_PART)
            p = jnp.exp2(s_ref[slot, keys, :] - m_new).astype(BF16)
            acc[0] = acc[0] + jnp.dot(vt_ref[c, :, keys], p, preferred_element_type=F32)

        if nxt is not None:
            scores(nxt[0], nxt[1], 1 - slot, pv_part)
        else:
            for j in range(nparts):
                pv_part(j)
        acc_ref[...] = acc[0]
        m_ref[...] = jnp.broadcast_to(m_new, m_ref.shape)

    group = 4 if nck % 4 == 0 else 2
    lam = (jnp.exp(jnp.sum(lq1_ref[...] * lk1_ref[...], axis=-1, keepdims=True))
           - jnp.exp(jnp.sum(lq2_ref[...] * lk2_ref[...], axis=-1, keepdims=True)) + lambda_init)
    scores(0, 0, 0, lambda j: None)
    for sub in range(nsub):
        m_ref[...] = jnp.full(m_ref.shape, NEG_BIG, F32)
        acc_ref[...] = jnp.zeros(acc_ref.shape, F32)

        def trip(c0, last, sub=sub):
            for u in range(group):
                c = c0 + u
                if not (last and u == group - 1):
                    nxt = (sub, c + 1)
                else:
                    nxt = (sub + 1, 0) if sub + 1 < nsub else None
                step(sub, c, u % 2, nxt)

        def body(i, carry, trip=trip):
            trip(group * i, False)
            return carry

        lax.fori_loop(0, nck // group - 1, body, 0)
        trip(nck - group, True)

        acc = acc_ref[...]
        ot = acc[:A_V_DIM] / acc[A_V_DIM:A_V_DIM + 1]
        o = ot[:, :tq] - lam * ot[:, tq:]
        ms = jnp.mean(o * o, axis=0, keepdims=True)
        o = o * lax.rsqrt(ms + SUBLN_EPS)
        o_ref[sub * tq:(sub + 1) * tq, :] = (o.T * g_ref[...] * (1.0 - lambda_init)).astype(o_ref.dtype)


def _diff_attn(z, batch, seq, lq1, lk1, lq2, lk2, subln_g, lambda_init, tq, tk, nsub):
    tokens = batch * seq
    nq = seq // (tq * nsub)
    assert seq % (2 * tk) == 0
    vec = lambda n: pl.BlockSpec((1, n), lambda b, h, qi: (0, 0))
    kern = functools.partial(_diff_attn_kernel, tq=tq, tk=tk, seq=seq, nsub=nsub, lambda_init=lambda_init)
    return pl.pallas_call(
        kern,
        grid=(batch, A_HEADS, nq),
        in_specs=[
            pl.BlockSpec((nsub * tq, A_V_DIM), lambda b, h, qi: (b * nq + qi, h)),
            pl.BlockSpec((seq, A_V_DIM), lambda b, h, qi: (b, A_HEADS + h)),
            pl.BlockSpec((seq, A_V_DIM), lambda b, h, qi: (b, 2 * A_HEADS + h)),
            vec(A_QK_DIM), vec(A_QK_DIM), vec(A_QK_DIM), vec(A_QK_DIM), vec(A_V_DIM),
        ],
        out_specs=pl.BlockSpec((nsub * tq, A_V_DIM), lambda b, h, qi: (b * nq + qi, h)),
        out_shape=jax.ShapeDtypeStruct((tokens, A_HEADS * A_V_DIM), BF16),
        scratch_shapes=[
            pltpu.VMEM((nsub, A_V_DIM, 2 * tq), BF16),
            pltpu.VMEM((seq // tk, VT_ROWS, tk), BF16),
            pltpu.VMEM((8, 2 * tq), F32),
            pltpu.VMEM((VT_ROWS, 2 * tq), F32),
            pltpu.VMEM((2, tk, 2 * tq), F32),
            pltpu.VMEM((2, 8, 2 * tq), F32),
        ],
        compiler_params=_params(3),
        name="diff_attn",
    )(z, z, z, lq1, lk1, lq2, lk2, subln_g)


def _band_window(prev_ref, cur_ref, next_ref, base, lo, hi, per, hs):
    parts = []
    if lo < 0:
        parts.append(prev_ref[base + per + lo:base + per + min(hi, 0), hs])
    if hi > 0 and lo < per:
        parts.append(cur_ref[base + max(lo, 0):base + min(hi, per), hs])
    if hi > per:
        parts.append(next_ref[base + max(lo, per) - per:base + hi - per, hs])
    return parts[0] if len(parts) == 1 else jnp.concatenate(parts, axis=0)


def _band_attn_kernel(q_ref, kp_ref, kc_ref, kn_ref, vp_ref, vc_ref, vn_ref, o_ref, lse_ref, of_ref, lf_ref,
                      *, dilation, length, chains):
    t = pl.program_id(1)
    per = TILE // dilation
    sub = min(per, BAND_SUB)
    ncls = BAND_SUB // sub
    nkeys = sub + 2 * BAND_HALF
    shape = (ncls * sub, ncls * nkeys)
    rows = lax.broadcasted_iota(jnp.int32, shape, 0)
    cols = lax.broadcasted_iota(jnp.int32, shape, 1)
    row_cls, row_pos = rows // sub, rows % sub
    col_cls, col_pos = cols // nkeys, cols % nkeys
    in_band = (jnp.abs(col_pos - BAND_HALF - row_pos) <= BAND_HALF) & (row_cls == col_cls)

    def window(refs, classes, lo, hi, hs):
        parts = [_band_window(*refs, rho * per, lo, hi, per, hs) for rho in classes]
        return parts[0] if len(parts) == 1 else jnp.concatenate(parts, axis=0)

    for i in range(per // sub):
        kpos = t * per + i * sub - BAND_HALF + col_pos
        mask = in_band & (kpos >= 0) & (kpos < length)
        lo, hi = i * sub - BAND_HALF, i * sub + sub + BAND_HALF
        tiles = [(range(rho0, rho0 + ncls), h) for rho0 in range(0, dilation, ncls)
                 for h in range(B_HEADS_PER_GROUP)]
        for b0 in range(0, len(tiles), chains):
            batch = tiles[b0:b0 + chains]
            scores = []
            for classes, h in batch:
                hs = slice(h * B_HEAD_DIM, (h + 1) * B_HEAD_DIM)
                q = q_ref[classes[0] * per + i * sub:classes[0] * per + i * sub + ncls * sub, hs]
                k = window((kp_ref, kc_ref, kn_ref), classes, lo, hi, hs)
                scores.append(lax.dot_general(q, k, (((1,), (1,)), ((), ())), preferred_element_type=F32))
            stats = []
            for s in scores:
                s = jnp.where(mask, s, NEG_BIG)
                m = jnp.max(s, axis=-1, keepdims=True)
                p = jnp.exp(s - m)
                stats.append((m, jnp.sum(p, axis=-1, keepdims=True), p.astype(BF16)))
            for (classes, h), (m, l, p) in zip(batch, stats):
                hs = slice(h * B_HEAD_DIM, (h + 1) * B_HEAD_DIM)
                v = window((vp_ref, vc_ref, vn_ref), classes, lo, hi, hs)
                o = jnp.dot(p, v, preferred_element_type=F32) / l
                lse = jnp.broadcast_to(m + jnp.log(l), (ncls * sub, B_HEAD_DIM))
                for n, rho in enumerate(classes):
                    out_rows = pl.ds(i * sub * dilation + rho, sub, stride=dilation) if dilation > 1 \
                        else pl.ds(i * sub, sub)
                    of_ref[h, out_rows, :] = o[n * sub:(n + 1) * sub]
                    lf_ref[h, out_rows, :] = lse[n * sub:(n + 1) * sub]
    for h in range(B_HEADS_PER_GROUP):
        hs = slice(h * B_HEAD_DIM, (h + 1) * B_HEAD_DIM)
        o_ref[:, hs] = of_ref[h].astype(o_ref.dtype)
        lse_ref[:, hs] = lf_ref[h]


def _band_attn(z, batch, seq, group):
    dilation = B_GROUPS[group][1]
    assert B_GROUPS[group][0] // (2 * dilation) == BAND_HALF
    nt = seq // TILE

    def tile(blk, shift):
        return pl.BlockSpec((TILE, COL_BLK),
                            lambda b, t: (b * nt + jnp.clip(t + shift, 0, nt - 1), blk + group))

    out_spec = pl.BlockSpec((TILE, B_OUT_W), lambda b, t: (b * nt + t, 0))
    kern = functools.partial(_band_attn_kernel, dilation=dilation, length=seq // dilation,
                             chains=BAND_CHAINS[group])
    return pl.pallas_call(
        kern,
        grid=(batch, nt),
        in_specs=[tile(QB_BLK, 0), tile(KB_BLK, -1), tile(KB_BLK, 0), tile(KB_BLK, 1),
                  tile(VB_BLK, -1), tile(VB_BLK, 0), tile(VB_BLK, 1)],
        out_specs=[out_spec, out_spec],
        out_shape=[jax.ShapeDtypeStruct((batch * seq, B_OUT_W), BF16),
                   jax.ShapeDtypeStruct((batch * seq, B_OUT_W), F32)],
        scratch_shapes=[pltpu.VMEM((B_HEADS_PER_GROUP, TILE, B_HEAD_DIM), F32),
                        pltpu.VMEM((B_HEADS_PER_GROUP, TILE, B_HEAD_DIM), F32)],
        compiler_params=_params(2),
        name=f"band_attn_g{group}",
    )(z, z, z, z, z, z, z)


def _merge_out_kernel(x_ref, oa_ref, o0_ref, o1_ref, o2_ref, l0_ref, l1_ref, l2_ref, *rest):
    n_c = D_MODEL // COL_BLK
    ga_refs, gb_refs = rest[:n_c], rest[n_c:2 * n_c]
    wpa_ref, wpb_ref, wout_ref, y_ref = rest[2 * n_c:]
    l0, l1, l2 = l0_ref[...], l1_ref[...], l2_ref[...]
    mx = jnp.maximum(jnp.maximum(l0, l1), l2)
    e0, e1, e2 = jnp.exp(l0 - mx), jnp.exp(l1 - mx), jnp.exp(l2 - mx)
    num = (e0 * o0_ref[...].astype(F32) + e1 * o1_ref[...].astype(F32) + e2 * o2_ref[...].astype(F32))
    ob = (num / (e0 + e1 + e2)).astype(BF16)
    ya = jnp.dot(oa_ref[...], wpa_ref[...], preferred_element_type=F32)
    yb = jnp.dot(ob, wpb_ref[...], preferred_element_type=F32)
    merged = []
    for c in range(n_c):
        cs = slice(c * COL_BLK, (c + 1) * COL_BLK)
        merged.append((jax.nn.sigmoid(ga_refs[c][...].astype(F32)) * ya[:, cs]
                       + jax.nn.sigmoid(gb_refs[c][...].astype(F32)) * yb[:, cs]).astype(BF16))
    merged = jnp.concatenate(merged, axis=1)
    y_ref[...] = x_ref[...] + jnp.dot(merged, wout_ref[...], preferred_element_type=F32)


def _merge_out(x2d, oa, ob_groups, z, wpa, wpb, wout, tm):
    tokens = x2d.shape[0]
    row = lambda w: pl.BlockSpec((tm, w), lambda i: (i, 0))
    gate = lambda blk: pl.BlockSpec((tm, COL_BLK), lambda i: (i, blk))
    whole = lambda a: pl.BlockSpec(a.shape, lambda i: (0, 0), pipeline_mode=pl.Buffered(1))
    n_c = D_MODEL // COL_BLK
    (o0, l0), (o1, l1), (o2, l2) = ob_groups
    return pl.pallas_call(
        _merge_out_kernel,
        grid=(tokens // tm,),
        in_specs=[row(D_MODEL), row(A_HEADS * A_V_DIM)] + [row(B_OUT_W)] * 6
        + [gate(GA_BLK + c) for c in range(n_c)] + [gate(GB_BLK + c) for c in range(n_c)]
        + [whole(wpa), whole(wpb), whole(wout)],
        out_specs=row(D_MODEL),
        out_shape=jax.ShapeDtypeStruct((tokens, D_MODEL), F32),
        compiler_params=_params(1),
        name="merge_out",
    )(x2d, oa, o0, o1, o2, l0, l1, l2, *([z] * (2 * n_c)), wpa, wpb, wout)


def _ffn_kernel(x_ref, g_ref, w1_ref, w2_ref, gf_ref, y_ref, h_ref, acc_ref, *, final_norm):
    f = pl.program_id(1)

    @pl.when(f == 0)
    def _():
        x = x_ref[...]
        ms = jnp.mean(x * x, axis=-1, keepdims=True)
        h_ref[...] = (x * lax.rsqrt(ms + NORM_EPS) * g_ref[...]).astype(h_ref.dtype)
        acc_ref[...] = x

    u = jnp.dot(h_ref[...], w1_ref[...], preferred_element_type=F32)
    u = jnp.square(jnp.maximum(u, 0.0)).astype(BF16)
    acc_ref[...] += jnp.dot(u, w2_ref[...], preferred_element_type=F32)

    @pl.when(f == pl.num_programs(1) - 1)
    def _():
        y = acc_ref[...]
        if final_norm:
            ms = jnp.mean(y * y, axis=-1, keepdims=True)
            y = y * lax.rsqrt(ms + NORM_EPS) * gf_ref[...]
        y_ref[...] = y


def _ffn(x2d, g, w1, w2, g_final, final_norm, tm, tf):
    tokens = x2d.shape[0]
    kern = functools.partial(_ffn_kernel, final_norm=final_norm)
    return pl.pallas_call(
        kern,
        grid=(tokens // tm, D_FF // tf),
        in_specs=[
            pl.BlockSpec((tm, D_MODEL), lambda i, f: (i, 0)),
            pl.BlockSpec((1, D_MODEL), lambda i, f: (0, 0)),
            pl.BlockSpec((D_MODEL, tf), lambda i, f: (0, f)),
            pl.BlockSpec((tf, D_MODEL), lambda i, f: (f, 0)),
            pl.BlockSpec((1, D_MODEL), lambda i, f: (0, 0)),
        ],
        out_specs=pl.BlockSpec((tm, D_MODEL), lambda i, f: (i, 0)),
        out_shape=jax.ShapeDtypeStruct((tokens, D_MODEL), F32),
        scratch_shapes=[pltpu.VMEM((tm, D_MODEL), BF16), pltpu.VMEM((tm, D_MODEL), F32)],
        compiler_params=_params(2),
        name="ffn",
    )(x2d, g, w1, w2, g_final)


def _position_tables(seq):
    cos_a, sin_a = _rope_tables(seq, A_QK_DIM, (1,))
    return (cos_a[0], sin_a[0]), _rope_tables(seq, B_HEAD_DIM, [dilation for _, dilation in B_GROUPS])


def _trunk(x, tabs, norm_mix, w_in, lambda_q1, lambda_k1, lambda_q2, lambda_k2, subln_g,
           w_proj_a, w_proj_b, w_out, norm_ffn, w1, w2, norm_final):
    batch, seq, _ = x.shape
    assert seq % TILE == 0
    depth = w_in.shape[0]
    x2d = x.reshape(batch * seq, D_MODEL)
    tabs_a, tabs_b = tabs
    row = lambda v: v.reshape(1, -1)
    tq, tk, nsub = (1024, 512, 1) if seq <= 4096 else (512, 1024, 2)
    for l in range(depth):
        lambda_init = 0.8 - 0.6 * math.exp(-0.3 * l)
        z = _in_proj(x2d, seq, row(norm_mix[l]), w_in[l], tabs_a, tabs_b)
        oa = _diff_attn(z, batch, seq, row(lambda_q1[l]), row(lambda_k1[l]), row(lambda_q2[l]),
                        row(lambda_k2[l]), row(subln_g[l]), lambda_init, tq=tq, tk=tk, nsub=nsub)
        ob_groups = [_band_attn(z, batch, seq, g) for g in range(N_GROUPS)]
        x2d = _merge_out(x2d, oa, ob_groups, z, w_proj_a[l], w_proj_b[l], w_out[l], tm=256)
        x2d = _ffn(x2d, row(norm_ffn[l]), w1[l], w2[l], row(norm_final),
                   final_norm=(l == depth - 1), tm=512, tf=1024)
    return x2d.reshape(batch, seq, D_MODEL)


def kernel(x_prompt, x_sample, norm_mix, w_in, lambda_q1, lambda_k1, lambda_q2, lambda_k2, subln_g,
           w_proj_a, w_proj_b, w_out, norm_ffn, w1, w2, norm_final):
    weights = (norm_mix, _prep_w_in(w_in), lambda_q1, lambda_k1, lambda_q2,
               lambda_k2, subln_g, w_proj_a.astype(BF16), w_proj_b.astype(BF16), w_out.astype(BF16), norm_ffn,
               w1.astype(BF16), w2.astype(BF16), norm_final)
    tabs = _position_tables(max(x_prompt.shape[1], x_sample.shape[1]))
    return _trunk(x_prompt, tabs, *weights), _trunk(x_sample, tabs, *weights)
```

```python
import functools
import math

import jax
import jax.numpy as jnp
import numpy as np
from jax import lax
from jax.experimental import pallas as pl
from jax.experimental.pallas import tpu as pltpu

D_MODEL = 2048
A_HEADS = 8
A_QK_DIM = 64
A_V_DIM = 2 * A_QK_DIM
B_GROUPS = ((128, 1), (512, 4), (2048, 16))
B_HEADS_PER_GROUP = 4
B_HEAD_DIM = 128
B_OUT_W = B_HEADS_PER_GROUP * B_HEAD_DIM
D_FF = 4 * D_MODEL
ROPE_THETA = 500000.0
ROPE_FRACTION_DEN = 4
NORM_EPS = 1e-6
SUBLN_EPS = 1e-5
NEG_BIG = -1e30
LOG2E = 1.4426950408889634

COL_BLK = 512
IN_WIDTH = 11776
N_COL_BLKS = IN_WIDTH // COL_BLK
QA_BLK, KA_BLK, VA_BLK = 0, 2, 4
QB_BLK, KB_BLK, VB_BLK = 6, 9, 12
GA_BLK, GB_BLK = 15, 19
N_GROUPS = len(B_GROUPS)
LANES = 128
TILE = 1024
ROW_PARTS = 4
BAND_HALF = 64
BAND_SUB = 128
KEY_PART = 256
BAND_CHAINS = (1, 1, 16)
VT_ROWS = A_V_DIM + 16

VMEM_LIMIT = 56 * 1024 * 1024

BF16 = jnp.bfloat16
F32 = jnp.float32


def _params(n_axes):
    return pltpu.CompilerParams(dimension_semantics=("arbitrary",) * n_axes,
                                vmem_limit_bytes=VMEM_LIMIT)


def _rotary_layout(head_dim):
    half = head_dim // ROPE_FRACTION_DEN // 2
    heads = LANES // head_dim
    first = [h * head_dim + i for h in range(heads) for i in range(half)]
    second = [h * head_dim + half + i for h in range(heads) for i in range(half)]
    target1 = list(range(len(first)))
    target2 = list(range(LANES // 2, LANES // 2 + len(second)))
    src = np.arange(LANES)
    freq = np.full(LANES, -1)
    sign = np.zeros(LANES)
    for lanes, cols, sgn in ((target1, first, -1.0), (target2, second, 1.0)):
        for j, (lane, col) in enumerate(zip(lanes, cols)):
            src[lane], freq[lane], sign[lane] = col, j % half, sgn
    rotary, targets = set(first + second), set(target1 + target2)
    for lane, col in zip(sorted(rotary - targets), sorted(targets - rotary)):
        src[lane] = col
    assert sorted(src) == list(range(LANES))
    return src, freq, sign


def _runs(idx):
    out, start = [], 0
    for i in range(1, len(idx) + 1):
        if i == len(idx) or idx[i] != idx[i - 1] + 1:
            out.append((int(idx[start]), int(idx[i - 1]) + 1))
            start = i
    return out


def _block_layout(j):
    return jnp.where(j < VA_BLK, 1, jnp.where((j >= QB_BLK) & (j < VB_BLK), 2, 0))


def _prep_w_in_kernel(w_ref, perm_ref, o_ref):
    o_ref[...] = jnp.dot(w_ref[0].astype(BF16), perm_ref[...], preferred_element_type=F32).astype(o_ref.dtype)


def _prep_w_in(w_in):
    depth = w_in.shape[0]
    perms = []
    for head_dim in (None, A_QK_DIM, B_HEAD_DIM):
        src = np.arange(LANES) if head_dim is None else _rotary_layout(head_dim)[0]
        src = np.concatenate([src + g * LANES for g in range(COL_BLK // LANES)])
        perms.append(jnp.arange(COL_BLK)[:, None] == jnp.asarray(src)[None, :])
    perms = jnp.stack(perms).astype(BF16)
    return pl.pallas_call(
        _prep_w_in_kernel,
        grid=(depth, N_COL_BLKS),
        in_specs=[pl.BlockSpec((1, D_MODEL, COL_BLK), lambda l, j: (l, 0, j)),
                  pl.BlockSpec((None, COL_BLK, COL_BLK), lambda l, j: (_block_layout(j), 0, 0))],
        out_specs=pl.BlockSpec((None, None, D_MODEL, COL_BLK), lambda l, j: (l, j, 0, 0)),
        out_shape=jax.ShapeDtypeStruct((depth, N_COL_BLKS, D_MODEL, COL_BLK), BF16),
        compiler_params=_params(2),
        name="prep_w_in",
    )(w_in, perms)


def _rope_tables(seq, head_dim, dilations):
    rot = head_dim // ROPE_FRACTION_DEN
    inv = ROPE_THETA ** (-jnp.arange(0, rot, 2, dtype=F32) / rot)
    _, freq, sign = _rotary_layout(head_dim)
    inv_lane = jnp.where(jnp.asarray(freq >= 0), inv[np.maximum(freq, 0)], 0.0)[None, :]
    sign = jnp.asarray(sign, F32)[None, :]
    base = jnp.arange(0, seq, TILE, dtype=F32)[:, None] * inv_lane
    cos_a, sin_a = jnp.cos(base)[:, None, :], jnp.sin(base)[:, None, :]
    cos_tabs, sin_tabs = [], []
    for dilation in dilations:
        offset = np.arange(TILE).reshape(TILE // dilation, dilation).T.reshape(TILE)
        ang = jnp.asarray(offset, F32)[:, None] * inv_lane
        cos_b, sin_b = jnp.cos(ang)[None], jnp.sin(ang)[None]
        cos_tabs.append((cos_a * cos_b - sin_a * sin_b).reshape(seq, LANES))
        sin_tabs.append(((sin_a * cos_b + cos_a * sin_b) * sign).reshape(seq, LANES))
    return jnp.stack(cos_tabs), jnp.stack(sin_tabs)


def _rope_store(project, cos_ref, sin_ref, scale, z_ref):
    for r in range(ROW_PARTS):
        rows = slice(r * (TILE // ROW_PARTS), (r + 1) * (TILE // ROW_PARTS))
        acc, cos, sin = project(rows), cos_ref[rows, :], sin_ref[rows, :]
        for c in range(COL_BLK // LANES):
            xc = acc[:, c * LANES:(c + 1) * LANES]
            out = xc * cos + pltpu.roll(xc, LANES // 2, 1) * sin
            if scale != 1.0:
                out = out * scale
            z_ref[rows, c * LANES:(c + 1) * LANES] = out.astype(z_ref.dtype)


def _row_order(j):
    return jnp.where((j >= QB_BLK) & (j < GA_BLK), (j - QB_BLK) % N_GROUPS, 0)


def _in_proj_kernel(x_ref, g_ref, w_ref, ca_ref, sa_ref, cb_ref, sb_ref, z_ref, h_ref, slab_ref):
    j = pl.program_id(1)

    @pl.when(j == 0)
    def _():
        x = x_ref[...]
        rinv = lax.rsqrt(jnp.mean(x * x, axis=-1, keepdims=True) + NORM_EPS)
        for c in range(D_MODEL // LANES):
            cs = slice(c * LANES, (c + 1) * LANES)
            slab = x_ref[:, cs] * rinv * g_ref[:, cs]
            h_ref[0, :, cs] = slab.astype(h_ref.dtype)
            slab_ref[...] = slab
            for order in range(1, N_GROUPS):
                dilation = B_GROUPS[order][1]
                per = TILE // dilation
                for rho in range(dilation):
                    h_ref[order, rho * per:(rho + 1) * per, cs] = (
                        slab_ref[pl.ds(rho, per, stride=dilation), :].astype(h_ref.dtype))

    def project(rows):
        return jnp.dot(h_ref[_row_order(j), rows, :], w_ref[...], preferred_element_type=F32)

    @pl.when(j < KA_BLK)
    def _():
        _rope_store(project, ca_ref, sa_ref, A_QK_DIM ** -0.5 * LOG2E, z_ref)

    @pl.when((j >= KA_BLK) & (j < VA_BLK))
    def _():
        _rope_store(project, ca_ref, sa_ref, 1.0, z_ref)

    @pl.when((j >= QB_BLK) & (j < KB_BLK))
    def _():
        _rope_store(project, cb_ref, sb_ref, B_HEAD_DIM ** -0.5, z_ref)

    @pl.when((j >= KB_BLK) & (j < VB_BLK))
    def _():
        _rope_store(project, cb_ref, sb_ref, 1.0, z_ref)

    @pl.when(((j >= VA_BLK) & (j < QB_BLK)) | (j >= VB_BLK))
    def _():
        for r in range(ROW_PARTS):
            rows = slice(r * (TILE // ROW_PARTS), (r + 1) * (TILE // ROW_PARTS))
            z_ref[rows, :] = project(rows).astype(z_ref.dtype)


def _in_proj(x2d, seq, g, w_bf16, tabs_a, tabs_b):
    tokens = x2d.shape[0]
    nt = seq // TILE
    tab_a = pl.BlockSpec((TILE, LANES), lambda i, j: (i % nt, 0))
    tab_b = pl.BlockSpec((None, TILE, LANES), lambda i, j: (_row_order(j), i % nt, 0))
    return pl.pallas_call(
        _in_proj_kernel,
        grid=(tokens // TILE, N_COL_BLKS),
        in_specs=[
            pl.BlockSpec((TILE, D_MODEL), lambda i, j: (i, 0)),
            pl.BlockSpec((1, D_MODEL), lambda i, j: (0, 0)),
            pl.BlockSpec((None, D_MODEL, COL_BLK), lambda i, j: (j, 0, 0)),
            tab_a, tab_a, tab_b, tab_b,
        ],
        out_specs=pl.BlockSpec((TILE, COL_BLK), lambda i, j: (i, j)),
        out_shape=jax.ShapeDtypeStruct((tokens, IN_WIDTH), BF16),
        scratch_shapes=[pltpu.VMEM((N_GROUPS, TILE, D_MODEL), BF16), pltpu.VMEM((TILE, LANES), F32)],
        compiler_params=_params(2),
        name="in_proj",
    )(x2d, g, w_bf16, *tabs_a, *tabs_b)


_MAP0_LANES = _runs(np.flatnonzero(_rotary_layout(A_QK_DIM)[0] < A_QK_DIM))


def _diff_attn_kernel(q_ref, k_ref, v_ref, lq1_ref, lk1_ref, lq2_ref, lk2_ref, g_ref, o_ref,
                      q2t_ref, vt_ref, m_ref, acc_ref, s_ref, cmax_ref, *, tq, tk, seq, nsub, lambda_init):
    qi = pl.program_id(2)
    nck = seq // tk

    @pl.when(qi == 0)
    def _():
        def transpose_chunk(c, carry):
            rows = pl.ds(pl.multiple_of(c * tk, tk), tk)
            vt_ref[c, :A_V_DIM, :] = v_ref[rows, :].astype(F32).T.astype(BF16)
            sub = lax.broadcasted_iota(jnp.int32, (VT_ROWS - A_V_DIM, tk), 0)
            vt_ref[c, A_V_DIM:, :] = jnp.where(sub == 0, 1.0, 0.0).astype(BF16)
            return carry
        lax.fori_loop(0, nck, transpose_chunk, 0)

    dim = lax.broadcasted_iota(jnp.int32, (A_V_DIM, 1), 0)
    map0 = functools.reduce(jnp.logical_or, [(dim >= a) & (dim < b) for a, b in _MAP0_LANES])
    for sub in range(nsub):
        qt = q_ref[sub * tq:(sub + 1) * tq, :].astype(F32).T
        q2t_ref[sub, :, :tq] = jnp.where(map0, qt, 0.0).astype(BF16)
        q2t_ref[sub, :, tq:] = jnp.where(map0, 0.0, qt).astype(BF16)

    nparts = tk // KEY_PART

    def scores_part(sub, c, slot, j):
        rows = pl.ds(pl.multiple_of(c * tk, tk) + j * KEY_PART, KEY_PART)
        s = jnp.dot(k_ref[rows, :], q2t_ref[sub], preferred_element_type=F32)
        s_ref[slot, j * KEY_PART:(j + 1) * KEY_PART, :] = s
        return jnp.max(s, axis=0, keepdims=True)

    def scores(sub, c, slot, do):
        cmax = None
        for j in range(nparts):
            part_max = scores_part(sub, c, slot, j)
            cmax = part_max if cmax is None else jnp.maximum(cmax, part_max)
            do(j)
        cmax_ref[slot] = jnp.broadcast_to(cmax, cmax_ref.shape[1:])

    def step(sub, c, slot, nxt):
        m_old = m_ref[0:1, :]
        m_new = jnp.maximum(m_old, cmax_ref[slot, 0:1, :])
        acc = [jnp.exp2(m_old - m_new) * acc_ref[...]]

        def pv_part(j):
            keys = slice(j * KEY_PART, (j + 1) * KEY_PART)
            p = jnp.exp2(s_ref[slot, keys, :] - m_new).astype(BF16)
            acc[0] = acc[0] + jnp.dot(vt_ref[c, :, keys], p, preferred_element_type=F32)

        if nxt is not None:
            scores(nxt[0], nxt[1], 1 - slot, pv_part)
        else:
            for j in range(nparts):
                pv_part(j)
        acc_ref[...] = acc[0]
        m_ref[...] = jnp.broadcast_to(m_new, m_ref.shape)

    group = 4 if nck % 4 == 0 else 2
    lam = (jnp.exp(jnp.sum(lq1_ref[...] * lk1_ref[...], axis=-1, keepdims=True))
           - jnp.exp(jnp.sum(lq2_ref[...] * lk2_ref[...], axis=-1, keepdims=True)) + lambda_init)
    scores(0, 0, 0, lambda j: None)
    for sub in range(nsub):
        m_ref[...] = jnp.full(m_ref.shape, NEG_BIG, F32)
        acc_ref[...] = jnp.zeros(acc_ref.shape, F32)

        def trip(c0, last, sub=sub):
            for u in range(group):
                c = c0 + u
                if not (last and u == group - 1):
                    nxt = (sub, c + 1)
                else:
                    nxt = (sub + 1, 0) if sub + 1 < nsub else None
                step(sub, c, u % 2, nxt)

        def body(i, carry, trip=trip):
            trip(group * i, False)
            return carry

        lax.fori_loop(0, nck // group - 1, body, 0)
        trip(nck - group, True)

        acc = acc_ref[...]
        ot = acc[:A_V_DIM] / acc[A_V_DIM:A_V_DIM + 1]
        o = ot[:, :tq] - lam * ot[:, tq:]
        ms = jnp.mean(o * o, axis=0, keepdims=True)
        o = o * lax.rsqrt(ms + SUBLN_EPS)
        o_ref[sub * tq:(sub + 1) * tq, :] = (o.T * g_ref[...] * (1.0 - lambda_init)).astype(o_ref.dtype)


def _diff_attn(z, batch, seq, lq1, lk1, lq2, lk2, subln_g, lambda_init, tq, tk, nsub):
    tokens = batch * seq
    nq = seq // (tq * nsub)
    assert seq % (2 * tk) == 0
    vec = lambda n: pl.BlockSpec((1, n), lambda b, h, qi: (0, 0))
    kern = functools.partial(_diff_attn_kernel, tq=tq, tk=tk, seq=seq, nsub=nsub, lambda_init=lambda_init)
    return pl.pallas_call(
        kern,
        grid=(batch, A_HEADS, nq),
        in_specs=[
            pl.BlockSpec((nsub * tq, A_V_DIM), lambda b, h, qi: (b * nq + qi, h)),
            pl.BlockSpec((seq, A_V_DIM), lambda b, h, qi: (b, A_HEADS + h)),
            pl.BlockSpec((seq, A_V_DIM), lambda b, h, qi: (b, 2 * A_HEADS + h)),
            vec(A_QK_DIM), vec(A_QK_DIM), vec(A_QK_DIM), vec(A_QK_DIM), vec(A_V_DIM),
        ],
        out_specs=pl.BlockSpec((nsub * tq, A_V_DIM), lambda b, h, qi: (b * nq + qi, h)),
        out_shape=jax.ShapeDtypeStruct((tokens, A_HEADS * A_V_DIM), BF16),
        scratch_shapes=[
            pltpu.VMEM((nsub, A_V_DIM, 2 * tq), BF16),
            pltpu.VMEM((seq // tk, VT_ROWS, tk), BF16),
            pltpu.VMEM((8, 2 * tq), F32),
            pltpu.VMEM((VT_ROWS, 2 * tq), F32),
            pltpu.VMEM((2, tk, 2 * tq), F32),
            pltpu.VMEM((2, 8, 2 * tq), F32),
        ],
        compiler_params=_params(3),
        name="diff_attn",
    )(z, z, z, lq1, lk1, lq2, lk2, subln_g)


def _band_window(prev_ref, cur_ref, next_ref, base, lo, hi, per, hs):
    parts = []
    if lo < 0:
        parts.append(prev_ref[base + per + lo:base + per + min(hi, 0), hs])
    if hi > 0 and lo < per:
        parts.append(cur_ref[base + max(lo, 0):base + min(hi, per), hs])
    if hi > per:
        parts.append(next_ref[base + max(lo, per) - per:base + hi - per, hs])
    return parts[0] if len(parts) == 1 else jnp.concatenate(parts, axis=0)


def _band_attn_kernel(q_ref, kp_ref, kc_ref, kn_ref, vp_ref, vc_ref, vn_ref, o_ref, lse_ref, of_ref, lf_ref,
                      *, dilation, length, chains):
    t = pl.program_id(1)
    per = TILE // dilation
    sub = min(per, BAND_SUB)
    ncls = BAND_SUB // sub
    nkeys = sub + 2 * BAND_HALF
    shape = (ncls * sub, ncls * nkeys)
    rows = lax.broadcasted_iota(jnp.int32, shape, 0)
    cols = lax.broadcasted_iota(jnp.int32, shape, 1)
    row_cls, row_pos = rows // sub, rows % sub
    col_cls, col_pos = cols // nkeys, cols % nkeys
    in_band = (jnp.abs(col_pos - BAND_HALF - row_pos) <= BAND_HALF) & (row_cls == col_cls)

    def window(refs, classes, lo, hi, hs):
        parts = [_band_window(*refs, rho * per, lo, hi, per, hs) for rho in classes]
        return parts[0] if len(parts) == 1 else jnp.concatenate(parts, axis=0)

    for i in range(per // sub):
        kpos = t * per + i * sub - BAND_HALF + col_pos
        mask = in_band & (kpos >= 0) & (kpos < length)
        lo, hi = i * sub - BAND_HALF, i * sub + sub + BAND_HALF
        tiles = [(range(rho0, rho0 + ncls), h) for rho0 in range(0, dilation, ncls)
                 for h in range(B_HEADS_PER_GROUP)]
        for b0 in range(0, len(tiles), chains):
            batch = tiles[b0:b0 + chains]
            scores = []
            for classes, h in batch:
                hs = slice(h * B_HEAD_DIM, (h + 1) * B_HEAD_DIM)
                q = q_ref[classes[0] * per + i * sub:classes[0] * per + i * sub + ncls * sub, hs]
                k = window((kp_ref, kc_ref, kn_ref), classes, lo, hi, hs)
                scores.append(lax.dot_general(q, k, (((1,), (1,)), ((), ())), preferred_element_type=F32))
            stats = []
            for s in scores:
                s = jnp.where(mask, s, NEG_BIG)
                m = jnp.max(s, axis=-1, keepdims=True)
                p = jnp.exp(s - m)
                stats.append((m, jnp.sum(p, axis=-1, keepdims=True), p.astype(BF16)))
            for (classes, h), (m, l, p) in zip(batch, stats):
                hs = slice(h * B_HEAD_DIM, (h + 1) * B_HEAD_DIM)
                v = window((vp_ref, vc_ref, vn_ref), classes, lo, hi, hs)
                o = jnp.dot(p, v, preferred_element_type=F32) / l
                lse = jnp.broadcast_to(m + jnp.log(l), (ncls * sub, B_HEAD_DIM))
                for n, rho in enumerate(classes):
                    out_rows = pl.ds(i * sub * dilation + rho, sub, stride=dilation) if dilation > 1 \
                        else pl.ds(i * sub, sub)
                    of_ref[h, out_rows, :] = o[n * sub:(n + 1) * sub]
                    lf_ref[h, out_rows, :] = lse[n * sub:(n + 1) * sub]
    for h in range(B_HEADS_PER_GROUP):
        hs = slice(h * B_HEAD_DIM, (h + 1) * B_HEAD_DIM)
        o_ref[:, hs] = of_ref[h].astype(o_ref.dtype)
        lse_ref[:, hs] = lf_ref[h]


def _band_attn(z, batch, seq, group):
    dilation = B_GROUPS[group][1]
    assert B_GROUPS[group][0] // (2 * dilation) == BAND_HALF
    nt = seq // TILE

    def tile(blk, shift):
        return pl.BlockSpec((TILE, COL_BLK),
                            lambda b, t: (b * nt + jnp.clip(t + shift, 0, nt - 1), blk + group))

    out_spec = pl.BlockSpec((TILE, B_OUT_W), lambda b, t: (b * nt + t, 0))
    kern = functools.partial(_band_attn_kernel, dilation=dilation, length=seq // dilation,
                             chains=BAND_CHAINS[group])
    return pl.pallas_call(
        kern,
        grid=(batch, nt),
        in_specs=[tile(QB_BLK, 0), tile(KB_BLK, -1), tile(KB_BLK, 0), tile(KB_BLK, 1),
                  tile(VB_BLK, -1), tile(VB_BLK, 0), tile(VB_BLK, 1)],
        out_specs=[out_spec, out_spec],
        out_shape=[jax.ShapeDtypeStruct((batch * seq, B_OUT_W), BF16),
                   jax.ShapeDtypeStruct((batch * seq, B_OUT_W), F32)],
        scratch_shapes=[pltpu.VMEM((B_HEADS_PER_GROUP, TILE, B_HEAD_DIM), F32),
                        pltpu.VMEM((B_HEADS_PER_GROUP, TILE, B_HEAD_DIM), F32)],
        compiler_params=_params(2),
        name=f"band_attn_g{group}",
    )(z, z, z, z, z, z, z)


def _merge_out_kernel(x_ref, oa_ref, o0_ref, o1_ref, o2_ref, l0_ref, l1_ref, l2_ref, *rest):
    n_c = D_MODEL // COL_BLK
    ga_refs, gb_refs = rest[:n_c], rest[n_c:2 * n_c]
    wpa_ref, wpb_ref, wout_ref, y_ref = rest[2 * n_c:]
    l0, l1, l2 = l0_ref[...], l1_ref[...], l2_ref[...]
    mx = jnp.maximum(jnp.maximum(l0, l1), l2)
    e0, e1, e2 = jnp.exp(l0 - mx), jnp.exp(l1 - mx), jnp.exp(l2 - mx)
    num = (e0 * o0_ref[...].astype(F32) + e1 * o1_ref[...].astype(F32) + e2 * o2_ref[...].astype(F32))
    ob = (num / (e0 + e1 + e2)).astype(BF16)
    ya = jnp.dot(oa_ref[...], wpa_ref[...], preferred_element_type=F32)
    yb = jnp.dot(ob, wpb_ref[...], preferred_element_type=F32)
    merged = []
    for c in range(n_c):
        cs = slice(c * COL_BLK, (c + 1) * COL_BLK)
        merged.append((jax.nn.sigmoid(ga_refs[c][...].astype(F32)) * ya[:, cs]
                       + jax.nn.sigmoid(gb_refs[c][...].astype(F32)) * yb[:, cs]).astype(BF16))
    merged = jnp.concatenate(merged, axis=1)
    y_ref[...] = x_ref[...] + jnp.dot(merged, wout_ref[...], preferred_element_type=F32)


def _merge_out(x2d, oa, ob_groups, z, wpa, wpb, wout, tm):
    tokens = x2d.shape[0]
    row = lambda w: pl.BlockSpec((tm, w), lambda i: (i, 0))
    gate = lambda blk: pl.BlockSpec((tm, COL_BLK), lambda i: (i, blk))
    whole = lambda a: pl.BlockSpec(a.shape, lambda i: (0, 0), pipeline_mode=pl.Buffered(1))
    n_c = D_MODEL // COL_BLK
    (o0, l0), (o1, l1), (o2, l2) = ob_groups
    return pl.pallas_call(
        _merge_out_kernel,
        grid=(tokens // tm,),
        in_specs=[row(D_MODEL), row(A_HEADS * A_V_DIM)] + [row(B_OUT_W)] * 6
        + [gate(GA_BLK + c) for c in range(n_c)] + [gate(GB_BLK + c) for c in range(n_c)]
        + [whole(wpa), whole(wpb), whole(wout)],
        out_specs=row(D_MODEL),
        out_shape=jax.ShapeDtypeStruct((tokens, D_MODEL), F32),
        compiler_params=_params(1),
        name="merge_out",
    )(x2d, oa, o0, o1, o2, l0, l1, l2, *([z] * (2 * n_c)), wpa, wpb, wout)


def _ffn_kernel(x_ref, g_ref, w1_ref, w2_ref, gf_ref, y_ref, h_ref, acc_ref, *, final_norm):
    f = pl.program_id(1)

    @pl.when(f == 0)
    def _():
        x = x_ref[...]
        ms = jnp.mean(x * x, axis=-1, keepdims=True)
        h_ref[...] = (x * lax.rsqrt(ms + NORM_EPS) * g_ref[...]).astype(h_ref.dtype)
        acc_ref[...] = x

    u = jnp.dot(h_ref[...], w1_ref[...], preferred_element_type=F32)
    u = jnp.square(jnp.maximum(u, 0.0)).astype(BF16)
    acc_ref[...] += jnp.dot(u, w2_ref[...], preferred_element_type=F32)

    @pl.when(f == pl.num_programs(1) - 1)
    def _():
        y = acc_ref[...]
        if final_norm:
            ms = jnp.mean(y * y, axis=-1, keepdims=True)
            y = y * lax.rsqrt(ms + NORM_EPS) * gf_ref[...]
        y_ref[...] = y


def _ffn(x2d, g, w1, w2, g_final, final_norm, tm, tf):
    tokens = x2d.shape[0]
    kern = functools.partial(_ffn_kernel, final_norm=final_norm)
    return pl.pallas_call(
        kern,
        grid=(tokens // tm, D_FF // tf),
        in_specs=[
            pl.BlockSpec((tm, D_MODEL), lambda i, f: (i, 0)),
            pl.BlockSpec((1, D_MODEL), lambda i, f: (0, 0)),
            pl.BlockSpec((D_MODEL, tf), lambda i, f: (0, f)),
            pl.BlockSpec((tf, D_MODEL), lambda i, f: (f, 0)),
            pl.BlockSpec((1, D_MODEL), lambda i, f: (0, 0)),
        ],
        out_specs=pl.BlockSpec((tm, D_MODEL), lambda i, f: (i, 0)),
        out_shape=jax.ShapeDtypeStruct((tokens, D_MODEL), F32),
        scratch_shapes=[pltpu.VMEM((tm, D_MODEL), BF16), pltpu.VMEM((tm, D_MODEL), F32)],
        compiler_params=_params(2),
        name="ffn",
    )(x2d, g, w1, w2, g_final)


def _position_tables(seq):
    cos_a, sin_a = _rope_tables(seq, A_QK_DIM, (1,))
    return (cos_a[0], sin_a[0]), _rope_tables(seq, B_HEAD_DIM, [dilation for _, dilation in B_GROUPS])


def _trunk(x, tabs, norm_mix, w_in, lambda_q1, lambda_k1, lambda_q2, lambda_k2, subln_g,
           w_proj_a, w_proj_b, w_out, norm_ffn, w1, w2, norm_final):
    batch, seq, _ = x.shape
    assert seq % TILE == 0
    depth = w_in.shape[0]
    x2d = x.reshape(batch * seq, D_MODEL)
    tabs_a, tabs_b = tabs
    row = lambda v: v.reshape(1, -1)
    tq, tk, nsub = (1024, 512, 1) if seq <= 4096 else (512, 1024, 4)
    for l in range(depth):
        lambda_init = 0.8 - 0.6 * math.exp(-0.3 * l)
        z = _in_proj(x2d, seq, row(norm_mix[l]), w_in[l], tabs_a, tabs_b)
        oa = _diff_attn(z, batch, seq, row(lambda_q1[l]), row(lambda_k1[l]), row(lambda_q2[l]),
                        row(lambda_k2[l]), row(subln_g[l]), lambda_init, tq=tq, tk=tk, nsub=nsub)
        ob_groups = [_band_attn(z, batch, seq, g) for g in range(N_GROUPS)]
        x2d = _merge_out(x2d, oa, ob_groups, z, w_proj_a[l], w_proj_b[l], w_out[l], tm=256)
        x2d = _ffn(x2d, row(norm_ffn[l]), w1[l], w2[l], row(norm_final),
                   final_norm=(l == depth - 1), tm=512, tf=1024)
    return x2d.reshape(batch, seq, D_MODEL)


def kernel(x_prompt, x_sample, norm_mix, w_in, lambda_q1, lambda_k1, lambda_q2, lambda_k2, subln_g,
           w_proj_a, w_proj_b, w_out, norm_ffn, w1, w2, norm_final):
    weights = (norm_mix, _prep_w_in(w_in), lambda_q1, lambda_k1, lambda_q2,
               lambda_k2, subln_g, w_proj_a.astype(BF16), w_proj_b.astype(BF16), w_out.astype(BF16), norm_ffn,
               w1.astype(BF16), w2.astype(BF16), norm_final)
    tabs = _position_tables(max(x_prompt.shape[1], x_sample.shape[1]))
    return _trunk(x_prompt, tabs, *weights), _trunk(x_sample, tabs, *weights)
```

```python
import functools
import math

import jax
import jax.numpy as jnp
import numpy as np
from jax import lax
from jax.experimental import pallas as pl
from jax.experimental.pallas import tpu as pltpu

D_MODEL = 2048
A_HEADS = 8
A_QK_DIM = 64
A_V_DIM = 2 * A_QK_DIM
B_GROUPS = ((128, 1), (512, 4), (2048, 16))
B_HEADS_PER_GROUP = 4
B_HEAD_DIM = 128
B_OUT_W = B_HEADS_PER_GROUP * B_HEAD_DIM
D_FF = 4 * D_MODEL
ROPE_THETA = 500000.0
ROPE_FRACTION_DEN = 4
NORM_EPS = 1e-6
SUBLN_EPS = 1e-5
NEG_BIG = -1e30
LOG2E = 1.4426950408889634

COL_BLK = 512
IN_WIDTH = 11776
N_COL_BLKS = IN_WIDTH // COL_BLK
QA_BLK, KA_BLK, VA_BLK = 0, 2, 4
QB_BLK, KB_BLK, VB_BLK = 6, 9, 12
GA_BLK, GB_BLK = 15, 19
N_GROUPS = len(B_GROUPS)
LANES = 128
TILE = 1024
ROW_PARTS = 4
BAND_HALF = 64
BAND_SUB = 128
KEY_PART = 256
BAND_CHAINS = (1, 1, 16)
VT_ROWS = A_V_DIM + 16

VMEM_LIMIT = 56 * 1024 * 1024

BF16 = jnp.bfloat16
F32 = jnp.float32


def _params(n_axes):
    return pltpu.CompilerParams(dimension_semantics=("arbitrary",) * n_axes,
                                vmem_limit_bytes=VMEM_LIMIT)


def _rotary_layout(head_dim):
    half = head_dim // ROPE_FRACTION_DEN // 2
    heads = LANES // head_dim
    first = [h * head_dim + i for h in range(heads) for i in range(half)]
    second = [h * head_dim + half + i for h in range(heads) for i in range(half)]
    target1 = list(range(len(first)))
    target2 = list(range(LANES // 2, LANES // 2 + len(second)))
    src = np.arange(LANES)
    freq = np.full(LANES, -1)
    sign = np.zeros(LANES)
    for lanes, cols, sgn in ((target1, first, -1.0), (target2, second, 1.0)):
        for j, (lane, col) in enumerate(zip(lanes, cols)):
            src[lane], freq[lane], sign[lane] = col, j % half, sgn
    rotary, targets = set(first + second), set(target1 + target2)
    for lane, col in zip(sorted(rotary - targets), sorted(targets - rotary)):
        src[lane] = col
    assert sorted(src) == list(range(LANES))
    return src, freq, sign


def _runs(idx):
    out, start = [], 0
    for i in range(1, len(idx) + 1):
        if i == len(idx) or idx[i] != idx[i - 1] + 1:
            out.append((int(idx[start]), int(idx[i - 1]) + 1))
            start = i
    return out


def _block_layout(j):
    return jnp.where(j < VA_BLK, 1, jnp.where((j >= QB_BLK) & (j < VB_BLK), 2, 0))


def _prep_w_in_kernel(w_ref, perm_ref, o_ref):
    o_ref[...] = jnp.dot(w_ref[0].astype(BF16), perm_ref[...], preferred_element_type=F32).astype(o_ref.dtype)


def _prep_w_in(w_in):
    depth = w_in.shape[0]
    perms = []
    for head_dim in (None, A_QK_DIM, B_HEAD_DIM):
        src = np.arange(LANES) if head_dim is None else _rotary_layout(head_dim)[0]
        src = np.concatenate([src + g * LANES for g in range(COL_BLK // LANES)])
        perms.append(jnp.arange(COL_BLK)[:, None] == jnp.asarray(src)[None, :])
    perms = jnp.stack(perms).astype(BF16)
    return pl.pallas_call(
        _prep_w_in_kernel,
        grid=(depth, N_COL_BLKS),
        in_specs=[pl.BlockSpec((1, D_MODEL, COL_BLK), lambda l, j: (l, 0, j)),
                  pl.BlockSpec((None, COL_BLK, COL_BLK), lambda l, j: (_block_layout(j), 0, 0))],
        out_specs=pl.BlockSpec((None, None, D_MODEL, COL_BLK), lambda l, j: (l, j, 0, 0)),
        out_shape=jax.ShapeDtypeStruct((depth, N_COL_BLKS, D_MODEL, COL_BLK), BF16),
        compiler_params=_params(2),
        name="prep_w_in",
    )(w_in, perms)


def _rope_tables(seq, head_dim, dilations):
    rot = head_dim // ROPE_FRACTION_DEN
    inv = ROPE_THETA ** (-jnp.arange(0, rot, 2, dtype=F32) / rot)
    _, freq, sign = _rotary_layout(head_dim)
    inv_lane = jnp.where(jnp.asarray(freq >= 0), inv[np.maximum(freq, 0)], 0.0)[None, :]
    sign = jnp.asarray(sign, F32)[None, :]
    base = jnp.arange(0, seq, TILE, dtype=F32)[:, None] * inv_lane
    cos_a, sin_a = jnp.cos(base)[:, None, :], jnp.sin(base)[:, None, :]
    cos_tabs, sin_tabs = [], []
    for dilation in dilations:
        offset = np.arange(TILE).reshape(TILE // dilation, dilation).T.reshape(TILE)
        ang = jnp.asarray(offset, F32)[:, None] * inv_lane
        cos_b, sin_b = jnp.cos(ang)[None], jnp.sin(ang)[None]
        cos_tabs.append((cos_a * cos_b - sin_a * sin_b).reshape(seq, LANES))
        sin_tabs.append(((sin_a * cos_b + cos_a * sin_b) * sign).reshape(seq, LANES))
    return jnp.stack(cos_tabs), jnp.stack(sin_tabs)


def _rope_store(project, cos_ref, sin_ref, scale, z_ref):
    for r in range(ROW_PARTS):
        rows = slice(r * (TILE // ROW_PARTS), (r + 1) * (TILE // ROW_PARTS))
        acc, cos, sin = project(rows), cos_ref[rows, :], sin_ref[rows, :]
        for c in range(COL_BLK // LANES):
            xc = acc[:, c * LANES:(c + 1) * LANES]
            out = xc * cos + pltpu.roll(xc, LANES // 2, 1) * sin
            if scale != 1.0:
                out = out * scale
            z_ref[rows, c * LANES:(c + 1) * LANES] = out.astype(z_ref.dtype)


def _row_order(j):
    return jnp.where((j >= QB_BLK) & (j < GA_BLK), (j - QB_BLK) % N_GROUPS, 0)


def _in_proj_kernel(x_ref, g_ref, w_ref, ca_ref, sa_ref, cb_ref, sb_ref, z_ref, h_ref, slab_ref):
    j = pl.program_id(1)

    @pl.when(j == 0)
    def _():
        x = x_ref[...]
        rinv = lax.rsqrt(jnp.mean(x * x, axis=-1, keepdims=True) + NORM_EPS)
        for c in range(D_MODEL // LANES):
            cs = slice(c * LANES, (c + 1) * LANES)
            slab = x_ref[:, cs] * rinv * g_ref[:, cs]
            h_ref[0, :, cs] = slab.astype(h_ref.dtype)
            slab_ref[...] = slab
            for order in range(1, N_GROUPS):
                dilation = B_GROUPS[order][1]
                per = TILE // dilation
                for rho in range(dilation):
                    h_ref[order, rho * per:(rho + 1) * per, cs] = (
                        slab_ref[pl.ds(rho, per, stride=dilation), :].astype(h_ref.dtype))

    def project(rows):
        return jnp.dot(h_ref[_row_order(j), rows, :], w_ref[...], preferred_element_type=F32)

    @pl.when(j < KA_BLK)
    def _():
        _rope_store(project, ca_ref, sa_ref, A_QK_DIM ** -0.5 * LOG2E, z_ref)

    @pl.when((j >= KA_BLK) & (j < VA_BLK))
    def _():
        _rope_store(project, ca_ref, sa_ref, 1.0, z_ref)

    @pl.when((j >= QB_BLK) & (j < KB_BLK))
    def _():
        _rope_store(project, cb_ref, sb_ref, B_HEAD_DIM ** -0.5, z_ref)

    @pl.when((j >= KB_BLK) & (j < VB_BLK))
    def _():
        _rope_store(project, cb_ref, sb_ref, 1.0, z_ref)

    @pl.when(((j >= VA_BLK) & (j < QB_BLK)) | (j >= VB_BLK))
    def _():
        for r in range(ROW_PARTS):
            rows = slice(r * (TILE // ROW_PARTS), (r + 1) * (TILE // ROW_PARTS))
            z_ref[rows, :] = project(rows).astype(z_ref.dtype)


def _in_proj(x2d, seq, g, w_bf16, tabs_a, tabs_b):
    tokens = x2d.shape[0]
    nt = seq // TILE
    tab_a = pl.BlockSpec((TILE, LANES), lambda i, j: (i % nt, 0))
    tab_b = pl.BlockSpec((None, TILE, LANES), lambda i, j: (_row_order(j), i % nt, 0))
    return pl.pallas_call(
        _in_proj_kernel,
        grid=(tokens // TILE, N_COL_BLKS),
        in_specs=[
            pl.BlockSpec((TILE, D_MODEL), lambda i, j: (i, 0), pipeline_mode=pl.Buffered(1)),
            pl.BlockSpec((1, D_MODEL), lambda i, j: (0, 0)),
            pl.BlockSpec((None, D_MODEL, COL_BLK), lambda i, j: (j, 0, 0)),
            tab_a, tab_a, tab_b, tab_b,
        ],
        out_specs=pl.BlockSpec((TILE, COL_BLK), lambda i, j: (i, j)),
        out_shape=jax.ShapeDtypeStruct((tokens, IN_WIDTH), BF16),
        scratch_shapes=[pltpu.VMEM((N_GROUPS, TILE, D_MODEL), BF16), pltpu.VMEM((TILE, LANES), F32)],
        compiler_params=_params(2),
        name="in_proj",
    )(x2d, g, w_bf16, *tabs_a, *tabs_b)


_MAP0_LANES = _runs(np.flatnonzero(_rotary_layout(A_QK_DIM)[0] < A_QK_DIM))


def _diff_attn_kernel(q_ref, k_ref, v_ref, lq1_ref, lk1_ref, lq2_ref, lk2_ref, g_ref, o_ref,
                      q2t_ref, vt_ref, m_ref, acc_ref, s_ref, cmax_ref, *, tq, tk, seq, nsub, lambda_init):
    qi = pl.program_id(2)
    nck = seq // tk

    @pl.when(qi == 0)
    def _():
        def transpose_chunk(c, carry):
            rows = pl.ds(pl.multiple_of(c * tk, tk), tk)
            vt_ref[c, :A_V_DIM, :] = v_ref[rows, :].astype(F32).T.astype(BF16)
            sub = lax.broadcasted_iota(jnp.int32, (VT_ROWS - A_V_DIM, tk), 0)
            vt_ref[c, A_V_DIM:, :] = jnp.where(sub == 0, 1.0, 0.0).astype(BF16)
            return carry
        lax.fori_loop(0, nck, transpose_chunk, 0)

    dim = lax.broadcasted_iota(jnp.int32, (A_V_DIM, 1), 0)
    map0 = functools.reduce(jnp.logical_or, [(dim >= a) & (dim < b) for a, b in _MAP0_LANES])
    for sub in range(nsub):
        qt = q_ref[sub * tq:(sub + 1) * tq, :].astype(F32).T
        q2t_ref[sub, :, :tq] = jnp.where(map0, qt, 0.0).astype(BF16)
        q2t_ref[sub, :, tq:] = jnp.where(map0, 0.0, qt).astype(BF16)

    nparts = tk // KEY_PART

    def scores_part(sub, c, slot, j):
        rows = pl.ds(pl.multiple_of(c * tk, tk) + j * KEY_PART, KEY_PART)
        s = jnp.dot(k_ref[rows, :], q2t_ref[sub], preferred_element_type=F32)
        s_ref[slot, j * KEY_PART:(j + 1) * KEY_PART, :] = s
        return jnp.max(s, axis=0, keepdims=True)

    def scores(sub, c, slot, do):
        cmax = None
        for j in range(nparts):
            part_max = scores_part(sub, c, slot, j)
            cmax = part_max if cmax is None else jnp.maximum(cmax, part_max)
            do(j)
        cmax_ref[slot] = jnp.broadcast_to(cmax, cmax_ref.shape[1:])

    def step(sub, c, slot, nxt):
        m_old = m_ref[0:1, :]
        m_new = jnp.maximum(m_old, cmax_ref[slot, 0:1, :])
        acc = [jnp.exp2(m_old - m_new) * acc_ref[...]]

        def pv_part(j):
            keys = slice(j * KEY_PART, (j + 1) * KEY_PART)
            p = jnp.exp2(s_ref[slot, keys, :] - m_new).astype(BF16)
            acc[0] = acc[0] + jnp.dot(vt_ref[c, :, keys], p, preferred_element_type=F32)

        if nxt is not None:
            scores(nxt[0], nxt[1], 1 - slot, pv_part)
        else:
            for j in range(nparts):
                pv_part(j)
        acc_ref[...] = acc[0]
        m_ref[...] = jnp.broadcast_to(m_new, m_ref.shape)

    group = 4 if nck % 4 == 0 else 2
    lam = (jnp.exp(jnp.sum(lq1_ref[...] * lk1_ref[...], axis=-1, keepdims=True))
           - jnp.exp(jnp.sum(lq2_ref[...] * lk2_ref[...], axis=-1, keepdims=True)) + lambda_init)
    scores(0, 0, 0, lambda j: None)
    for sub in range(nsub):
        m_ref[...] = jnp.full(m_ref.shape, NEG_BIG, F32)
        acc_ref[...] = jnp.zeros(acc_ref.shape, F32)

        def trip(c0, last, sub=sub):
            for u in range(group):
                c = c0 + u
                if not (last and u == group - 1):
                    nxt = (sub, c + 1)
                else:
                    nxt = (sub + 1, 0) if sub + 1 < nsub else None
                step(sub, c, u % 2, nxt)

        def body(i, carry, trip=trip):
            trip(group * i, False)
            return carry

        lax.fori_loop(0, nck // group - 1, body, 0)
        trip(nck - group, True)

        acc = acc_ref[...]
        ot = acc[:A_V_DIM] / acc[A_V_DIM:A_V_DIM + 1]
        o = ot[:, :tq] - lam * ot[:, tq:]
        ms = jnp.mean(o * o, axis=0, keepdims=True)
        o = o * lax.rsqrt(ms + SUBLN_EPS)
        o_ref[sub * tq:(sub + 1) * tq, :] = (o.T * g_ref[...] * (1.0 - lambda_init)).astype(o_ref.dtype)


def _diff_attn(z, batch, seq, lq1, lk1, lq2, lk2, subln_g, lambda_init, tq, tk, nsub):
    tokens = batch * seq
    nq = seq // (tq * nsub)
    assert seq % (2 * tk) == 0
    vec = lambda n: pl.BlockSpec((1, n), lambda b, h, qi: (0, 0))
    kern = functools.partial(_diff_attn_kernel, tq=tq, tk=tk, seq=seq, nsub=nsub, lambda_init=lambda_init)
    return pl.pallas_call(
        kern,
        grid=(batch, A_HEADS, nq),
        in_specs=[
            pl.BlockSpec((nsub * tq, A_V_DIM), lambda b, h, qi: (b * nq + qi, h)),
            pl.BlockSpec((seq, A_V_DIM), lambda b, h, qi: (b, A_HEADS + h)),
            pl.BlockSpec((seq, A_V_DIM), lambda b, h, qi: (b, 2 * A_HEADS + h)),
            vec(A_QK_DIM), vec(A_QK_DIM), vec(A_QK_DIM), vec(A_QK_DIM), vec(A_V_DIM),
        ],
        out_specs=pl.BlockSpec((nsub * tq, A_V_DIM), lambda b, h, qi: (b * nq + qi, h)),
        out_shape=jax.ShapeDtypeStruct((tokens, A_HEADS * A_V_DIM), BF16),
        scratch_shapes=[
            pltpu.VMEM((nsub, A_V_DIM, 2 * tq), BF16),
            pltpu.VMEM((seq // tk, VT_ROWS, tk), BF16),
            pltpu.VMEM((8, 2 * tq), F32),
            pltpu.VMEM((VT_ROWS, 2 * tq), F32),
            pltpu.VMEM((2, tk, 2 * tq), F32),
            pltpu.VMEM((2, 8, 2 * tq), F32),
        ],
        compiler_params=_params(3),
        name="diff_attn",
    )(z, z, z, lq1, lk1, lq2, lk2, subln_g)


def _band_window(prev_ref, cur_ref, next_ref, base, lo, hi, per, hs):
    parts = []
    if lo < 0:
        parts.append(prev_ref[base + per + lo:base + per + min(hi, 0), hs])
    if hi > 0 and lo < per:
        parts.append(cur_ref[base + max(lo, 0):base + min(hi, per), hs])
    if hi > per:
        parts.append(next_ref[base + max(lo, per) - per:base + hi - per, hs])
    return parts[0] if len(parts) == 1 else jnp.concatenate(parts, axis=0)


def _band_attn_kernel(q_ref, kp_ref, kc_ref, kn_ref, vp_ref, vc_ref, vn_ref, o_ref, lse_ref, of_ref, lf_ref,
                      *, dilation, length, chains):
    t = pl.program_id(1)
    per = TILE // dilation
    sub = min(per, BAND_SUB)
    ncls = BAND_SUB // sub
    nkeys = sub + 2 * BAND_HALF
    shape = (ncls * sub, ncls * nkeys)
    rows = lax.broadcasted_iota(jnp.int32, shape, 0)
    cols = lax.broadcasted_iota(jnp.int32, shape, 1)
    row_cls, row_pos = rows // sub, rows % sub
    col_cls, col_pos = cols // nkeys, cols % nkeys
    in_band = (jnp.abs(col_pos - BAND_HALF - row_pos) <= BAND_HALF) & (row_cls == col_cls)

    def window(refs, classes, lo, hi, hs):
        parts = [_band_window(*refs, rho * per, lo, hi, per, hs) for rho in classes]
        return parts[0] if len(parts) == 1 else jnp.concatenate(parts, axis=0)

    for i in range(per // sub):
        kpos = t * per + i * sub - BAND_HALF + col_pos
        mask = in_band & (kpos >= 0) & (kpos < length)
        lo, hi = i * sub - BAND_HALF, i * sub + sub + BAND_HALF
        tiles = [(range(rho0, rho0 + ncls), h) for rho0 in range(0, dilation, ncls)
                 for h in range(B_HEADS_PER_GROUP)]
        for b0 in range(0, len(tiles), chains):
            batch = tiles[b0:b0 + chains]
            scores = []
            for classes, h in batch:
                hs = slice(h * B_HEAD_DIM, (h + 1) * B_HEAD_DIM)
                q = q_ref[classes[0] * per + i * sub:classes[0] * per + i * sub + ncls * sub, hs]
                k = window((kp_ref, kc_ref, kn_ref), classes, lo, hi, hs)
                scores.append(lax.dot_general(q, k, (((1,), (1,)), ((), ())), preferred_element_type=F32))
            stats = []
            for s in scores:
                s = jnp.where(mask, s, NEG_BIG)
                m = jnp.max(s, axis=-1, keepdims=True)
                p = jnp.exp(s - m)
                stats.append((m, jnp.sum(p, axis=-1, keepdims=True), p.astype(BF16)))
            for (classes, h), (m, l, p) in zip(batch, stats):
                hs = slice(h * B_HEAD_DIM, (h + 1) * B_HEAD_DIM)
                v = window((vp_ref, vc_ref, vn_ref), classes, lo, hi, hs)
                o = jnp.dot(p, v, preferred_element_type=F32) / l
                lse = jnp.broadcast_to(m + jnp.log(l), (ncls * sub, B_HEAD_DIM))
                for n, rho in enumerate(classes):
                    out_rows = pl.ds(i * sub * dilation + rho, sub, stride=dilation) if dilation > 1 \
                        else pl.ds(i * sub, sub)
                    of_ref[h, out_rows, :] = o[n * sub:(n + 1) * sub]
                    lf_ref[h, out_rows, :] = lse[n * sub:(n + 1) * sub]
    for h in range(B_HEADS_PER_GROUP):
        hs = slice(h * B_HEAD_DIM, (h + 1) * B_HEAD_DIM)
        o_ref[:, hs] = of_ref[h].astype(o_ref.dtype)
        lse_ref[:, hs] = lf_ref[h]


def _band_attn(z, batch, seq, group):
    dilation = B_GROUPS[group][1]
    assert B_GROUPS[group][0] // (2 * dilation) == BAND_HALF
    nt = seq // TILE

    def tile(blk, shift):
        return pl.BlockSpec((TILE, COL_BLK),
                            lambda b, t: (b * nt + jnp.clip(t + shift, 0, nt - 1), blk + group))

    out_spec = pl.BlockSpec((TILE, B_OUT_W), lambda b, t: (b * nt + t, 0))
    kern = functools.partial(_band_attn_kernel, dilation=dilation, length=seq // dilation,
                             chains=BAND_CHAINS[group])
    return pl.pallas_call(
        kern,
        grid=(batch, nt),
        in_specs=[tile(QB_BLK, 0), tile(KB_BLK, -1), tile(KB_BLK, 0), tile(KB_BLK, 1),
                  tile(VB_BLK, -1), tile(VB_BLK, 0), tile(VB_BLK, 1)],
        out_specs=[out_spec, out_spec],
        out_shape=[jax.ShapeDtypeStruct((batch * seq, B_OUT_W), BF16),
                   jax.ShapeDtypeStruct((batch * seq, B_OUT_W), F32)],
        scratch_shapes=[pltpu.VMEM((B_HEADS_PER_GROUP, TILE, B_HEAD_DIM), F32),
                        pltpu.VMEM((B_HEADS_PER_GROUP, TILE, B_HEAD_DIM), F32)],
        compiler_params=_params(2),
        name=f"band_attn_g{group}",
    )(z, z, z, z, z, z, z)


def _merge_out_kernel(x_ref, oa_ref, o0_ref, o1_ref, o2_ref, l0_ref, l1_ref, l2_ref, *rest):
    n_c = D_MODEL // COL_BLK
    ga_refs, gb_refs = rest[:n_c], rest[n_c:2 * n_c]
    wpa_ref, wpb_ref, wout_ref, y_ref = rest[2 * n_c:]
    l0, l1, l2 = l0_ref[...], l1_ref[...], l2_ref[...]
    mx = jnp.maximum(jnp.maximum(l0, l1), l2)
    e0, e1, e2 = jnp.exp(l0 - mx), jnp.exp(l1 - mx), jnp.exp(l2 - mx)
    num = (e0 * o0_ref[...].astype(F32) + e1 * o1_ref[...].astype(F32) + e2 * o2_ref[...].astype(F32))
    ob = (num / (e0 + e1 + e2)).astype(BF16)
    ya = jnp.dot(oa_ref[...], wpa_ref[...], preferred_element_type=F32)
    yb = jnp.dot(ob, wpb_ref[...], preferred_element_type=F32)
    merged = []
    for c in range(n_c):
        cs = slice(c * COL_BLK, (c + 1) * COL_BLK)
        merged.append((jax.nn.sigmoid(ga_refs[c][...].astype(F32)) * ya[:, cs]
                       + jax.nn.sigmoid(gb_refs[c][...].astype(F32)) * yb[:, cs]).astype(BF16))
    merged = jnp.concatenate(merged, axis=1)
    y_ref[...] = x_ref[...] + jnp.dot(merged, wout_ref[...], preferred_element_type=F32)


def _merge_out(x2d, oa, ob_groups, z, wpa, wpb, wout, tm):
    tokens = x2d.shape[0]
    row = lambda w: pl.BlockSpec((tm, w), lambda i: (i, 0))
    gate = lambda blk: pl.BlockSpec((tm, COL_BLK), lambda i: (i, blk))
    whole = lambda a: pl.BlockSpec(a.shape, lambda i: (0, 0), pipeline_mode=pl.Buffered(1))
    n_c = D_MODEL // COL_BLK
    (o0, l0), (o1, l1), (o2, l2) = ob_groups
    return pl.pallas_call(
        _merge_out_kernel,
        grid=(tokens // tm,),
        in_specs=[row(D_MODEL), row(A_HEADS * A_V_DIM)] + [row(B_OUT_W)] * 6
        + [gate(GA_BLK + c) for c in range(n_c)] + [gate(GB_BLK + c) for c in range(n_c)]
        + [whole(wpa), whole(wpb), whole(wout)],
        out_specs=row(D_MODEL),
        out_shape=jax.ShapeDtypeStruct((tokens, D_MODEL), F32),
        compiler_params=_params(1),
        name="merge_out",
    )(x2d, oa, o0, o1, o2, l0, l1, l2, *([z] * (2 * n_c)), wpa, wpb, wout)


def _ffn_kernel(x_ref, g_ref, w1_ref, w2_ref, gf_ref, y_ref, h_ref, acc_ref, *, final_norm):
    f = pl.program_id(1)

    @pl.when(f == 0)
    def _():
        x = x_ref[...]
        ms = jnp.mean(x * x, axis=-1, keepdims=True)
        h_ref[...] = (x * lax.rsqrt(ms + NORM_EPS) * g_ref[...]).astype(h_ref.dtype)
        acc_ref[...] = x

    u = jnp.dot(h_ref[...], w1_ref[...], preferred_element_type=F32)
    u = jnp.square(jnp.maximum(u, 0.0)).astype(BF16)
    acc_ref[...] += jnp.dot(u, w2_ref[...], preferred_element_type=F32)

    @pl.when(f == pl.num_programs(1) - 1)
    def _():
        y = acc_ref[...]
        if final_norm:
            ms = jnp.mean(y * y, axis=-1, keepdims=True)
            y = y * lax.rsqrt(ms + NORM_EPS) * gf_ref[...]
        y_ref[...] = y


def _ffn(x2d, g, w1, w2, g_final, final_norm, tm, tf):
    tokens = x2d.shape[0]
    kern = functools.partial(_ffn_kernel, final_norm=final_norm)
    return pl.pallas_call(
        kern,
        grid=(tokens // tm, D_FF // tf),
        in_specs=[
            pl.BlockSpec((tm, D_MODEL), lambda i, f: (i, 0)),
            pl.BlockSpec((1, D_MODEL), lambda i, f: (0, 0)),
            pl.BlockSpec((D_MODEL, tf), lambda i, f: (0, f)),
            pl.BlockSpec((tf, D_MODEL), lambda i, f: (f, 0)),
            pl.BlockSpec((1, D_MODEL), lambda i, f: (0, 0)),
        ],
        out_specs=pl.BlockSpec((tm, D_MODEL), lambda i, f: (i, 0)),
        out_shape=jax.ShapeDtypeStruct((tokens, D_MODEL), F32),
        scratch_shapes=[pltpu.VMEM((tm, D_MODEL), BF16), pltpu.VMEM((tm, D_MODEL), F32)],
        compiler_params=_params(2),
        name="ffn",
    )(x2d, g, w1, w2, g_final)


def _position_tables(seq):
    cos_a, sin_a = _rope_tables(seq, A_QK_DIM, (1,))
    return (cos_a[0], sin_a[0]), _rope_tables(seq, B_HEAD_DIM, [dilation for _, dilation in B_GROUPS])


def _trunk(x, tabs, norm_mix, w_in, lambda_q1, lambda_k1, lambda_q2, lambda_k2, subln_g,
           w_proj_a, w_proj_b, w_out, norm_ffn, w1, w2, norm_final):
    batch, seq, _ = x.shape
    assert seq % TILE == 0
    depth = w_in.shape[0]
    x2d = x.reshape(batch * seq, D_MODEL)
    tabs_a, tabs_b = tabs
    row = lambda v: v.reshape(1, -1)
    tq, tk, nsub = (1024, 512, 1) if seq <= 4096 else (512, 1024, 2)
    for l in range(depth):
        lambda_init = 0.8 - 0.6 * math.exp(-0.3 * l)
        z = _in_proj(x2d, seq, row(norm_mix[l]), w_in[l], tabs_a, tabs_b)
        oa = _diff_attn(z, batch, seq, row(lambda_q1[l]), row(lambda_k1[l]), row(lambda_q2[l]),
                        row(lambda_k2[l]), row(subln_g[l]), lambda_init, tq=tq, tk=tk, nsub=nsub)
        ob_groups = [_band_attn(z, batch, seq, g) for g in range(N_GROUPS)]
        x2d = _merge_out(x2d, oa, ob_groups, z, w_proj_a[l], w_proj_b[l], w_out[l], tm=256)
        x2d = _ffn(x2d, row(norm_ffn[l]), w1[l], w2[l], row(norm_final),
                   final_norm=(l == depth - 1), tm=512, tf=1024)
    return x2d.reshape(batch, seq, D_MODEL)


def kernel(x_prompt, x_sample, norm_mix, w_in, lambda_q1, lambda_k1, lambda_q2, lambda_k2, subln_g,
           w_proj_a, w_proj_b, w_out, norm_ffn, w1, w2, norm_final):
    weights = (norm_mix, _prep_w_in(w_in), lambda_q1, lambda_k1, lambda_q2,
               lambda_k2, subln_g, w_proj_a.astype(BF16), w_proj_b.astype(BF16), w_out.astype(BF16), norm_ffn,
               w1.astype(BF16), w2.astype(BF16), norm_final)
    tabs = _position_tables(max(x_prompt.shape[1], x_sample.shape[1]))
    return _trunk(x_prompt, tabs, *weights), _trunk(x_sample, tabs, *weights)
```
